```python
import math
import jax, jax.numpy as jnp
from jax import lax
import numpy as np

D_MODEL = 1024
BATCH = 4
SEQ = 4096
DEPTH = 2
DEC_BATCH = 32
DEC_SEQ = 1
PAST_LEN = 16384
PAGE_SIZE = 128

HEAD_DIM = 64
D_CONV = D_MODEL // 4
N_HEADS_FOX = 6
N_HEADS_MOBA = 6
D_FOX = N_HEADS_FOX * HEAD_DIM
D_MOBA = N_HEADS_MOBA * HEAD_DIM
D_MIX = D_CONV + D_FOX + D_MOBA
CONV_WIDTH = 31
FOX_BLOCK_Q = 128
MOBA_BLOCK = 256
MOBA_TOPK = 3
MOBA_BLOCK_Q = 64
N_BUCKETS = 32
MAX_DISTANCE = 128
D_FF = 2816
N_EXPERTS = 8
TOP_K = 2
D_FF_EXPERT = 7 * D_MODEL // 2
N_DENSE = (DEPTH + 1) // 2
N_MOE = DEPTH // 2
OFF_FOX = 2 * D_CONV
OFF_FGATE = OFF_FOX + 3 * D_FOX
OFF_MOBA = OFF_FGATE + N_HEADS_FOX
N_IN = OFF_MOBA + 3 * D_MOBA

kernel_name = 'hybrid_conv_fox_moba_step'


def rmsnorm(x, g, eps=1e-6):
    xf = x.astype(jnp.float32)
    return (xf * lax.rsqrt(jnp.mean(xf * xf, -1, keepdims=True) + eps)).astype(x.dtype) * g


def layernorm(x, g, b, eps=1e-5):
    xf = x.astype(jnp.float32)
    mu = jnp.mean(xf, -1, keepdims=True)
    var = jnp.mean(jnp.square(xf - mu), -1, keepdims=True)
    return ((xf - mu) * lax.rsqrt(var + eps)).astype(x.dtype) * g + b


def swiglu(h, wg, wu, wd):
    return (jax.nn.silu(h @ wg) * (h @ wu)) @ wd


def t5_bucket(d):
    d = jnp.maximum(d, 0)
    max_exact = N_BUCKETS // 2
    large = max_exact + (jnp.log(jnp.maximum(d, max_exact).astype(jnp.float32) / max_exact)
                         / math.log(MAX_DISTANCE / max_exact) * (N_BUCKETS - max_exact)).astype(jnp.int32)
    return jnp.where(d < max_exact, d, jnp.minimum(large, N_BUCKETS - 1))


def sweep_queries(fn, block, qpos, *qs):
    t = qpos.shape[0]
    if t <= block or t % block:
        return fn(qpos, *qs)
    nb = t // block

    def split(a):
        return jnp.moveaxis(a.reshape(a.shape[0], nb, block, *a.shape[2:]), 1, 0)

    out = lax.map(lambda args: fn(*args), (qpos.reshape(nb, block),) + tuple(split(a) for a in qs))
    out = jnp.moveaxis(out, 0, 1)
    return out.reshape(out.shape[0], t, *out.shape[3:])


def conv_module(u_in, prefix, w, b, ln_g, ln_b, w_pw):
    u = u_in[..., :D_CONV] * jax.nn.sigmoid(u_in[..., D_CONV:])
    up = jnp.concatenate([prefix.astype(u.dtype), u], axis=1)
    y = lax.conv_general_dilated(up, w[:, None, :], window_strides=(1,), padding='VALID',
                                 dimension_numbers=('NWC', 'WIO', 'NWC'),
                                 feature_group_count=D_CONV) + b
    y = jax.nn.silu(layernorm(y, ln_g, ln_b))
    return y @ w_pw, up[:, -(CONV_WIDTH - 1):]


def fox_attention(q, k, v, logf, past_len):
    t = q.shape[1]
    L = k.shape[1]
    scale = HEAD_DIM ** -0.5
    c = jnp.cumsum(logf, axis=1)
    c_k = jnp.moveaxis(c, 2, 1)[:, :, None, :]
    c_q = c[:, past_len:]
    kpos = jnp.arange(L)
    qpos = past_len + jnp.arange(t)

    def block_fn(qp, qb, cqb):
        s = jnp.einsum('bqhd,bkhd->bhqk', qb, k).astype(jnp.float32) * scale
        s = s + jnp.moveaxis(cqb, 2, 1)[..., None] - c_k
        s = jnp.where(kpos[None, :] <= qp[:, None], s, -jnp.inf)
        p = jax.nn.softmax(s, axis=-1).astype(v.dtype)
        return jnp.einsum('bhqk,bkhd->bqhd', p, v)

    return sweep_queries(block_fn, FOX_BLOCK_Q, qpos, q, c_q)


def moba_attention(q, k, v, rel_bias, past_len):
    b, L, h, hd = k.shape
    t = q.shape[1]
    nb = -(-L // MOBA_BLOCK)
    lp = nb * MOBA_BLOCK
    pad = ((0, 0), (0, lp - L), (0, 0), (0, 0))
    kp = jnp.pad(k, pad)
    vp = jnp.pad(v, pad)
    kmean = kp.reshape(b, nb, MOBA_BLOCK, h, hd).astype(jnp.float32).mean(2).astype(k.dtype)
    n_sel = min(MOBA_TOPK, nb)
    scale = HEAD_DIM ** -0.5
    bi = jnp.arange(b)[:, None, None, None]
    hi = jnp.arange(h)[None, :, None, None]
    table_h = rel_bias.T.astype(jnp.float32)
    offs = jnp.arange(MOBA_BLOCK)
    blocks = jnp.arange(nb)
    qpos = past_len + jnp.arange(t)

    def block_fn(qp, qb):
        c = qp.shape[0]
        own = qp // MOBA_BLOCK
        gate = jnp.einsum('bqhd,bnhd->bhqn', qb, kmean).astype(jnp.float32)
        gate = jnp.where(blocks[None, :] < own[:, None], gate, -jnp.inf)
        gv, idx = lax.top_k(gate, n_sel)
        valid = jnp.repeat(gv > -jnp.inf, MOBA_BLOCK, axis=-1)
        sel_pos = (idx[..., None] * MOBA_BLOCK + offs).reshape(b, h, c, n_sel * MOBA_BLOCK)
        k_sel = kp[bi, sel_pos, hi]
        v_sel = vp[bi, sel_pos, hi]
        own_pos = own[:, None] * MOBA_BLOCK + offs
        k_own = kp[:, own_pos]
        v_own = vp[:, own_pos]
        s_sel = jnp.einsum('bqhd,bhqsd->bhqs', qb, k_sel).astype(jnp.float32) * scale
        s_sel = s_sel + table_h[hi, t5_bucket(qp[None, None, :, None] - sel_pos)]
        s_sel = jnp.where(valid, s_sel, -jnp.inf)
        s_own = jnp.einsum('bqhd,bqshd->bhqs', qb, k_own).astype(jnp.float32) * scale
        s_own = s_own + table_h[:, t5_bucket(qp[:, None] - own_pos)][None]
        s_own = jnp.where((own_pos <= qp[:, None])[None, None], s_own, -jnp.inf)
        p = jax.nn.softmax(jnp.concatenate([s_sel, s_own], axis=-1), axis=-1).astype(v.dtype)
        ns = s_sel.shape[-1]
        return (jnp.einsum('bhqs,bhqsd->bqhd', p[..., :ns], v_sel)
                + jnp.einsum('bhqs,bqshd->bqhd', p[..., ns:], v_own))

    return sweep_queries(block_fn, MOBA_BLOCK_Q, qpos, q)


def setup_inputs(seed: int = 0) -> dict:
    key = jax.random.key(seed)
    ks = jax.random.split(key, 32)
    f32 = jnp.float32
    n_pages = PAST_LEN // PAGE_SIZE
    n_used = DEC_BATCH * n_pages
    n_pool = n_used + n_used // 4

    def nrm(k, shape, scale=1.0):
        return jax.random.normal(k, shape, f32) * scale

    page_table = jax.random.permutation(ks[0], n_pool)[:n_used].reshape(DEC_BATCH, n_pages).astype(jnp.int32)
    kv_f = (DEPTH, n_pool, PAGE_SIZE, N_HEADS_FOX, HEAD_DIM)
    kv_m = (DEPTH, n_pool, PAGE_SIZE, N_HEADS_MOBA, HEAD_DIM)
    return {
        'x_prompt': nrm(ks[1], (BATCH, SEQ, D_MODEL)),
        'x_sample': nrm(ks[2], (DEC_BATCH, DEC_SEQ, D_MODEL)),
        'cache_fox_k': nrm(ks[3], kv_f),
        'cache_fox_v': nrm(ks[4], kv_f),
        'cache_fox_logf': jax.nn.log_sigmoid(2.0 + nrm(ks[5], (DEPTH, n_pool, PAGE_SIZE, N_HEADS_FOX))),
        'cache_moba_k': nrm(ks[6], kv_m),
        'cache_moba_v': nrm(ks[7], kv_m),
        'state_conv': nrm(ks[8], (DEPTH, DEC_BATCH, CONV_WIDTH - 1, D_CONV), 0.5),
        'page_table': page_table,
        'norm_mix': 1.0 + nrm(ks[9], (DEPTH, D_MODEL), 0.01),
        'norm_ffn': 1.0 + nrm(ks[10], (DEPTH, D_MODEL), 0.01),
        'norm_final': 1.0 + nrm(ks[11], (D_MODEL,), 0.01),
        'w_in': nrm(ks[12], (DEPTH, D_MODEL, N_IN), D_MODEL ** -0.5),
        'b_forget': nrm(ks[13], (DEPTH, N_HEADS_FOX), 0.1),
        'conv_w': nrm(ks[14], (DEPTH, CONV_WIDTH, D_CONV), CONV_WIDTH ** -0.5),
        'conv_b': nrm(ks[15], (DEPTH, D_CONV), 0.01),
        'conv_ln_g': 1.0 + nrm(ks[16], (DEPTH, D_CONV), 0.01),
        'conv_ln_b': nrm(ks[17], (DEPTH, D_CONV), 0.01),
        'w_conv_pw': nrm(ks[18], (DEPTH, D_CONV, D_CONV), D_CONV ** -0.5),
        'rel_bias': nrm(ks[19], (N_BUCKETS, N_HEADS_MOBA), 0.2),
        'w_out': nrm(ks[20], (DEPTH, D_MIX, D_MODEL), D_MIX ** -0.5),
        'w_dense_gate': nrm(ks[21], (N_DENSE, D_MODEL, D_FF), D_MODEL ** -0.5),
        'w_dense_up': nrm(ks[22], (N_DENSE, D_MODEL, D_FF), D_MODEL ** -0.5),
        'w_dense_down': nrm(ks[23], (N_DENSE, D_FF, D_MODEL), D_FF ** -0.5),
        'w_router': nrm(ks[24], (N_MOE, D_MODEL, N_EXPERTS), D_MODEL ** -0.5),
        'w_moe_gate': nrm(ks[25], (N_MOE, N_EXPERTS, D_MODEL, D_FF_EXPERT), D_MODEL ** -0.5),
        'w_moe_up': nrm(ks[26], (N_MOE, N_EXPERTS, D_MODEL, D_FF_EXPERT), D_MODEL ** -0.5),
        'w_moe_down': nrm(ks[27], (N_MOE, N_EXPERTS, D_FF_EXPERT, D_MODEL), D_FF_EXPERT ** -0.5),
    }


def reference(x_prompt, x_sample, cache_fox_k, cache_fox_v, cache_fox_logf, cache_moba_k, cache_moba_v,
              state_conv, page_table, norm_mix, norm_ffn, norm_final, w_in, b_forget, conv_w, conv_b,
              conv_ln_g, conv_ln_b, w_conv_pw, rel_bias, w_out, w_dense_gate, w_dense_up, w_dense_down,
              w_router, w_moe_gate, w_moe_up, w_moe_down):
    past_len = page_table.shape[1] * PAGE_SIZE

    def gather(cache_l):
        pages = cache_l[page_table]
        return pages.reshape(pages.shape[0], past_len, *pages.shape[3:])

    def mixer(hn, l, past):
        fk_p, fv_p, flogf_p, mk_p, mv_p, conv_prefix = past
        b, t, _ = hn.shape
        p0 = fk_p.shape[1]
        proj = hn @ w_in[l]
        conv_out, conv_state = conv_module(proj[..., :OFF_FOX], conv_prefix, conv_w[l], conv_b[l],
                                           conv_ln_g[l], conv_ln_b[l], w_conv_pw[l])
        fqkv = proj[..., OFF_FOX:OFF_FGATE].reshape(b, t, 3, N_HEADS_FOX, HEAD_DIM)
        fq, fk, fv = fqkv[:, :, 0], fqkv[:, :, 1], fqkv[:, :, 2]
        flogf = jax.nn.log_sigmoid(proj[..., OFF_FGATE:OFF_MOBA].astype(jnp.float32)
                                   + b_forget[l].astype(jnp.float32))
        fox_out = fox_attention(fq, jnp.concatenate([fk_p, fk], 1), jnp.concatenate([fv_p, fv], 1),
                                jnp.concatenate([flogf_p.astype(jnp.float32), flogf], 1), p0)
        mqkv = proj[..., OFF_MOBA:].reshape(b, t, 3, N_HEADS_MOBA, HEAD_DIM)
        mq, mk, mv = mqkv[:, :, 0], mqkv[:, :, 1], mqkv[:, :, 2]
        moba_out = moba_attention(mq, jnp.concatenate([mk_p, mk], 1), jnp.concatenate([mv_p, mv], 1),
                                  rel_bias, p0)
        mixed = jnp.concatenate([conv_out, fox_out.reshape(b, t, D_FOX), moba_out.reshape(b, t, D_MOBA)], -1)
        return mixed @ w_out[l], (fk, fv, flogf, mk, mv, conv_state)

    def ffn(h, l):
        i = l // 2
        if l % 2 == 0:
            return swiglu(h, w_dense_gate[i], w_dense_up[i], w_dense_down[i])
        logits = (h @ w_router[i]).astype(jnp.float32)
        top_v, top_i = lax.top_k(logits, TOP_K)
        gates = jax.nn.softmax(top_v, axis=-1)
        combine = jnp.sum(jax.nn.one_hot(top_i, N_EXPERTS, dtype=jnp.float32) * gates[..., None], axis=-2)
        out = jnp.zeros_like(h)
        for e in range(N_EXPERTS):
            out = out + combine[..., e:e + 1].astype(h.dtype) * swiglu(h, w_moe_gate[i, e], w_moe_up[i, e], w_moe_down[i, e])
        return out

    def trunk(x, past_fn):
        states = []
        for l in range(DEPTH):
            mix, st = mixer(rmsnorm(x, norm_mix[l]), l, past_fn(l))
            x = x + mix
            x = x + ffn(rmsnorm(x, norm_ffn[l]), l)
            states.append(st)
        return rmsnorm(x, norm_final), [jnp.stack(s) for s in zip(*states)]

    bp = x_prompt.shape[0]
    dt = x_prompt.dtype
    empty_f = jnp.zeros((bp, 0, N_HEADS_FOX, HEAD_DIM), dt)
    empty_m = jnp.zeros((bp, 0, N_HEADS_MOBA, HEAD_DIM), dt)
    prompt_past = (empty_f, empty_f, jnp.zeros((bp, 0, N_HEADS_FOX), jnp.float32), empty_m, empty_m,
                   jnp.zeros((bp, CONV_WIDTH - 1, D_CONV), dt))
    y_prompt, st_p = trunk(x_prompt, lambda l: prompt_past)
    fox_k_p, fox_v_p, fox_logf_p, moba_k_p, moba_v_p, conv_p = st_p

    def sample_past(l):
        return (gather(cache_fox_k[l]), gather(cache_fox_v[l]), gather(cache_fox_logf[l]),
                gather(cache_moba_k[l]), gather(cache_moba_v[l]), state_conv[l])

    y_sample, st_s = trunk(x_sample, sample_past)
    fox_k_s, fox_v_s, fox_logf_s, moba_k_s, moba_v_s, conv_s = st_s
    return (y_prompt, y_sample, fox_k_p, fox_v_p, fox_logf_p, moba_k_p, moba_v_p, conv_p,
            fox_k_s, fox_v_s, fox_logf_s, moba_k_s, moba_v_s, conv_s)
```

```python
import functools
import math

import jax
import jax.numpy as jnp
from jax import lax
from jax.experimental import pallas as pl
from jax.experimental.pallas import tpu as pltpu

f32 = jnp.float32
bf16 = jnp.bfloat16

D_MODEL = 1024
HEAD_DIM = 64
N_HEADS = 6
D_ATT = N_HEADS * HEAD_DIM
D_CONV = 256
CONV_WIDTH = 31
PAGE_SIZE = 128
MOBA_BLOCK = 256
MOBA_TOPK = 3
N_BUCKETS = 32
MAX_DISTANCE = 128
N_EXPERTS = 8
Q_SCALE = HEAD_DIM ** -0.5

LANES = 128
SUBLANES = 8
VMEM_LIMIT = 56 * 1024 * 1024

ROW_TILE = 256
ATT_TILE = 256
CONV_TILE = 512
CONV_HALO = 32
FFN_ROW_TILE = 1024
FFN_COL_TILE = 256
DEC_PAGES = 8


def _cp(sem):
    return pltpu.CompilerParams(dimension_semantics=sem, vmem_limit_bytes=VMEM_LIMIT)


def _rms(x, g):
    return (x * lax.rsqrt(jnp.mean(x * x, axis=-1, keepdims=True) + 1e-6)) * g


def _log_sigmoid(x):
    return jnp.minimum(x, 0.0) - jnp.log(1.0 + jnp.exp(-jnp.abs(x)))


def _silu(x):
    return x * jax.nn.sigmoid(x)


def _round_bf16(x):
    return x.astype(bf16).astype(f32)


def _dot(a, b):
    return jnp.dot(a, b, preferred_element_type=f32)


def _dot_nt(a, b, precision=None):
    return lax.dot_general(a, b, (((1,), (1,)), ((), ())), preferred_element_type=f32, precision=precision)


def _t5_bucket(d):
    max_exact = N_BUCKETS // 2
    df = jnp.maximum(d, max_exact).astype(f32)
    large = max_exact + (jnp.log(df / max_exact) / math.log(MAX_DISTANCE / max_exact)
                         * (N_BUCKETS - max_exact)).astype(jnp.int32)
    return jnp.where(d < max_exact, d, jnp.minimum(large, N_BUCKETS - 1))


def _proj_body(with_t, x_ref, g_ref, wglu_ref, wf_ref, wm_ref, wg_ref, bfg_ref,
               u_ref, ff_ref, fm_ref, bff_ref, bfm_ref, lf_ref, *maybe_lft):
    hb = _rms(x_ref[...], g_ref[...]).astype(bf16)
    glu = _dot(hb, wglu_ref[...])
    u_ref[...] = glu[:, :D_CONV] * jax.nn.sigmoid(glu[:, D_CONV:])
    col = lax.broadcasted_iota(jnp.int32, (1, 3 * D_ATT), 1)
    qscale = jnp.where(col < D_ATT, Q_SCALE, 1.0).astype(f32)
    pf = _dot(hb, wf_ref[...])
    ff_ref[...] = pf
    bff_ref[...] = (pf * qscale).astype(bf16)
    pm = _dot(hb, wm_ref[...])
    fm_ref[...] = pm
    bfm_ref[...] = (pm * qscale).astype(bf16)
    lf = _log_sigmoid(_dot(hb, wg_ref[...]) + bfg_ref[...])
    lf_ref[...] = lf
    if with_t:
        maybe_lft[0][...] = lf.T[:SUBLANES]


def _proj(x, g, wglu, wf, wm, wg, bfg, with_t):
    m = x.shape[0]
    tm = min(ROW_TILE, m)
    row = lambda w: pl.BlockSpec((tm, w), lambda i: (i, 0))
    full = lambda a: pl.BlockSpec(a.shape, lambda i: (0, 0))
    out_shape = [jax.ShapeDtypeStruct((m, D_CONV), f32),
                 jax.ShapeDtypeStruct((m, 3 * D_ATT), f32), jax.ShapeDtypeStruct((m, 3 * D_ATT), f32),
                 jax.ShapeDtypeStruct((m, 3 * D_ATT), bf16), jax.ShapeDtypeStruct((m, 3 * D_ATT), bf16),
                 jax.ShapeDtypeStruct((m, LANES), f32)]
    out_specs = [row(D_CONV), row(3 * D_ATT), row(3 * D_ATT), row(3 * D_ATT), row(3 * D_ATT), row(LANES)]
    if with_t:
        out_shape.append(jax.ShapeDtypeStruct((SUBLANES, m), f32))
        out_specs.append(pl.BlockSpec((SUBLANES, tm), lambda i: (0, i)))
    return pl.pallas_call(
        functools.partial(_proj_body, with_t),
        grid=(m // tm,),
        in_specs=[row(D_MODEL), full(g), full(wglu), full(wf), full(wm), full(wg), full(bfg)],
        out_specs=out_specs, out_shape=out_shape,
        compiler_params=_cp(("parallel",)), name="proj")(x, g, wglu, wf, wm, wg, bfg)


def _cumsum_body(lf_ref, lft_ref, ccol_ref, crow_ref):
    x = lf_ref[...]
    t = x.shape[0]
    pos = lax.broadcasted_iota(jnp.int32, x.shape, 0)
    s = 1
    while s < t:
        x = x + jnp.where(pos >= s, pltpu.roll(x, s, axis=0), 0.0)
        s *= 2
    ccol_ref[...] = x
    y = lft_ref[...]
    pos = lax.broadcasted_iota(jnp.int32, y.shape, 1)
    s = 1
    while s < t:
        y = y + jnp.where(pos >= s, pltpu.roll(y, s, axis=1), 0.0)
        s *= 2
    crow_ref[...] = y


def _cumsum(lf, lft, b, t):
    return pl.pallas_call(
        _cumsum_body, grid=(b,),
        in_specs=[pl.BlockSpec((t, LANES), lambda i: (i, 0)), pl.BlockSpec((SUBLANES, t), lambda i: (0, i))],
        out_specs=[pl.BlockSpec((t, LANES), lambda i: (i, 0)), pl.BlockSpec((SUBLANES, t), lambda i: (0, i))],
        out_shape=[jax.ShapeDtypeStruct(lf.shape, f32), jax.ShapeDtypeStruct(lft.shape, f32)],
        compiler_params=_cp(("parallel",)), name="cumsum")(lf, lft)


def _conv_tail(y, lng_ref, lnb_ref, wpw_ref):
    mu = jnp.mean(y, axis=-1, keepdims=True)
    var = jnp.mean(jnp.square(y - mu), axis=-1, keepdims=True)
    z = _silu((y - mu) * lax.rsqrt(var + 1e-5) * lng_ref[...] + lnb_ref[...])
    return _dot(z.astype(bf16), wpw_ref[...])


def _conv_body(main_ref, halo_ref, w_ref, b_ref, lng_ref, lnb_ref, wpw_ref, o_ref):
    tt = main_ref.shape[0]
    win = _round_bf16(jnp.concatenate([main_ref[...], halo_ref[...]], axis=0))
    w = _round_bf16(w_ref[...])
    lead = CONV_HALO - (CONV_WIDTH - 1)
    acc = jnp.zeros((tt, D_CONV), f32) + b_ref[...]
    for k in range(CONV_WIDTH):
        acc = acc + win[lead + k:lead + k + tt] * w[k:k + 1]
    o_ref[...] = _conv_tail(acc, lng_ref, lnb_ref, wpw_ref)


def _conv_prompt(up, w, b, lng, lnb, wpw, t):
    bsz = up.shape[0]
    tt = min(CONV_TILE, t)
    full = lambda a: pl.BlockSpec(a.shape, lambda i, j: (0, 0))
    return pl.pallas_call(
        _conv_body, grid=(bsz, t // tt),
        in_specs=[pl.BlockSpec((None, tt, D_CONV), lambda i, j: (i, j, 0)),
                  pl.BlockSpec((None, CONV_HALO, D_CONV), lambda i, j: (i, (j + 1) * (tt // CONV_HALO), 0)),
                  full(w), full(b), full(lng), full(lnb), full(wpw)],
        out_specs=pl.BlockSpec((None, tt, D_CONV), lambda i, j: (i, j, 0)),
        out_shape=jax.ShapeDtypeStruct((bsz, t, D_CONV), f32),
        compiler_params=_cp(("parallel", "parallel")), name="conv_prompt")(up, up, w, b, lng, lnb, wpw)


def _conv_step_body(st_ref, u_ref, w_ref, b_ref, lng_ref, lnb_ref, wpw_ref, o_ref):
    w = _round_bf16(w_ref[...])
    acc = _round_bf16(u_ref[...]) * w[CONV_WIDTH - 1:CONV_WIDTH] + b_ref[...]
    for k in range(CONV_WIDTH - 1):
        acc = acc + _round_bf16(st_ref[k]) * w[k:k + 1]
    o_ref[...] = _conv_tail(acc, lng_ref, lnb_ref, wpw_ref)


def _conv_step(state_t, u, w, b, lng, lnb, wpw):
    return pl.pallas_call(
        _conv_step_body, out_shape=jax.ShapeDtypeStruct(u.shape, f32), name="conv_step")(state_t, u, w, b, lng, lnb, wpw)


def _flash_step(qh, k, v, bias, mask, carry):
    m, l, acc = carry
    s = _dot_nt(qh, k) + bias
    if mask is not None:
        s = jnp.where(mask, s, -jnp.inf)
    m_new = jnp.maximum(m, jnp.max(s, axis=-1, keepdims=True))
    alpha = jnp.exp(m - m_new)
    p = jnp.exp(s - m_new)
    l = alpha * l + jnp.sum(p, axis=-1, keepdims=True)
    acc = alpha * acc + _dot(p.astype(bf16), v)
    return m_new, l, acc


def _flash_init(tq):
    return (jnp.full((tq, 1), -jnp.inf, f32), jnp.zeros((tq, 1), f32), jnp.zeros((tq, LANES), f32))


def _fox_body(q_ref, k_ref, v_ref, ccol_ref, crow_ref, o_ref):
    tq = q_ref.shape[0]
    i = pl.program_id(1)
    lane = lax.broadcasted_iota(jnp.int32, (1, LANES), 1)
    r = lax.broadcasted_iota(jnp.int32, (tq, tq), 0)
    c = lax.broadcasted_iota(jnp.int32, (tq, tq), 1)
    causal = c <= r
    for j in range(N_HEADS // 2):
        cols = slice(j * LANES, (j + 1) * LANES)
        qp = q_ref[:, cols]
        outs = []
        for hh in range(2):
            h = 2 * j + hh
            in_head = (lane >= hh * HEAD_DIM) & (lane < (hh + 1) * HEAD_DIM)
            qh = jnp.where(in_head, qp, jnp.zeros_like(qp))
            cq = ccol_ref[:, h:h + 1]

            def tile(n, carry, mask, cols=cols, h=h, qh=qh, cq=cq):
                rows = pl.ds(pl.multiple_of(n * tq, tq), tq)
                ck = crow_ref[n][h:h + 1, :]
                return _flash_step(qh, k_ref[rows, cols], v_ref[rows, cols], cq - ck, mask, carry)

            carry = lax.fori_loop(0, i, lambda n, cr, tile=tile: tile(n, cr, None), _flash_init(tq))
            m, l, acc = tile(i, carry, causal)
            outs.append(acc / l)
        o_ref[:, cols] = jnp.where(lane < HEAD_DIM, outs[0], outs[1])


def _fox_prompt(qkv, ccol, crow3, b, t):
    tq = min(ATT_TILE, t)
    nq = t // tq
    return pl.pallas_call(
        _fox_body, grid=(b, nq),
        in_specs=[pl.BlockSpec((tq, D_ATT), lambda i, j: (i * nq + j, 0)),
                  pl.BlockSpec((t, D_ATT), lambda i, j: (i, 1)),
                  pl.BlockSpec((t, D_ATT), lambda i, j: (i, 2)),
                  pl.BlockSpec((tq, LANES), lambda i, j: (i * nq + j, 0)),
                  pl.BlockSpec((nq, SUBLANES, tq), lambda i, j: (i, 0, 0))],
        out_specs=pl.BlockSpec((tq, D_ATT), lambda i, j: (i * nq + j, 0)),
        out_shape=jax.ShapeDtypeStruct((b * t, D_ATT), f32),
        compiler_params=_cp(("parallel", "arbitrary")), name="fox_prompt")(qkv, qkv, qkv, ccol, crow3)


def _kmean_body(k_ref, o_ref):
    t = k_ref.shape[0]
    k = k_ref[...].reshape(t // MOBA_BLOCK, MOBA_BLOCK, D_ATT)
    o_ref[...] = jnp.mean(k, axis=1)


def _kmean_prompt(mqkv_f32, b, t):
    nb = t // MOBA_BLOCK
    return pl.pallas_call(
        _kmean_body, grid=(b,),
        in_specs=[pl.BlockSpec((t, D_ATT), lambda i: (i, 1))],
        out_specs=pl.BlockSpec((None, nb, D_ATT), lambda i: (i, 0, 0)),
        out_shape=jax.ShapeDtypeStruct((b, nb, D_ATT), f32),
        compiler_params=_cp(("parallel",)), name="kmean_prompt")(mqkv_f32)


def _t5_tiles_body(tab_ref, o_ref):
    h = pl.program_id(0)
    r = lax.broadcasted_iota(jnp.int32, (MOBA_BLOCK, MOBA_BLOCK), 0)
    c = lax.broadcasted_iota(jnp.int32, (MOBA_BLOCK, MOBA_BLOCK), 1)
    for which in range(2):
        bucket = _t5_bucket(jnp.maximum(r - c + which * MOBA_BLOCK, 0))
        acc = jnp.zeros((MOBA_BLOCK, MOBA_BLOCK), f32)
        for nbk in range(N_BUCKETS):
            acc = jnp.where(bucket == nbk, tab_ref[nbk, h], acc)
        o_ref[which] = acc


def _t5_tiles(rel_bias):
    return pl.pallas_call(
        _t5_tiles_body, grid=(N_HEADS,),
        in_specs=[pl.BlockSpec(memory_space=pltpu.SMEM)],
        out_specs=pl.BlockSpec((None, 2, MOBA_BLOCK, MOBA_BLOCK), lambda h: (h, 0, 0, 0)),
        out_shape=jax.ShapeDtypeStruct((N_HEADS, 2, MOBA_BLOCK, MOBA_BLOCK), f32),
        compiler_params=_cp(("parallel",)), name="t5_tiles")(rel_bias)


def _top_blocks(gate, n_valid):
    blk = lax.broadcasted_iota(jnp.int32, gate.shape, 1)
    nb = gate.shape[1]
    g = jnp.where(blk < n_valid, gate, -jnp.inf)
    sel = jnp.zeros(gate.shape, f32)
    for _ in range(min(MOBA_TOPK, nb)):
        mx = jnp.max(g, axis=-1, keepdims=True)
        idx = jnp.min(jnp.where(g == mx, blk, nb), axis=-1, keepdims=True)
        pick = (blk == idx) & (mx > -jnp.inf)
        sel = jnp.where(pick, 1.0, sel)
        g = jnp.where(blk == idx, -jnp.inf, g)
    return sel


def _moba_body(tab_ref, q_ref, k_ref, v_ref, kmean_ref, t5_ref, o_ref):
    tq = q_ref.shape[0]
    nb = kmean_ref.shape[0]
    i = pl.program_id(1)
    lane = lax.broadcasted_iota(jnp.int32, (1, LANES), 1)
    r = lax.broadcasted_iota(jnp.int32, (tq, tq), 0)
    c = lax.broadcasted_iota(jnp.int32, (tq, tq), 1)
    causal = c <= r
    blk = lax.broadcasted_iota(jnp.int32, (tq, nb), 1)
    prev = jnp.maximum(i - 1, 0)
    for j in range(N_HEADS // 2):
        cols = slice(j * LANES, (j + 1) * LANES)
        qp = q_ref[:, cols]
        kmp = kmean_ref[:, cols].astype(bf16)
        outs = []
        for hh in range(2):
            h = 2 * j + hh
            in_head = (lane >= hh * HEAD_DIM) & (lane < (hh + 1) * HEAD_DIM)
            qh = jnp.where(in_head, qp, jnp.zeros_like(qp))
            sel = _top_blocks(_dot_nt(qh, kmp), i)

            def picked(n, sel=sel):
                return jnp.max(jnp.where(blk == n, sel, 0.0), axis=-1, keepdims=True) > 0.5

            def tile(n, bias, mask, carry, cols=cols, qh=qh):
                rows = pl.ds(pl.multiple_of(n * tq, tq), tq)
                return _flash_step(qh, k_ref[rows, cols], v_ref[rows, cols], bias, mask, carry)

            carry = tile(i, t5_ref[h, 0], causal, _flash_init(tq))
            carry = tile(prev, t5_ref[h, 1], picked(prev), carry)
            far = tab_ref[N_BUCKETS - 1, h]
            carry = lax.fori_loop(0, prev, lambda n, cr, tile=tile, picked=picked, far=far: tile(n, far, picked(n), cr), carry)
            m, l, acc = carry
            outs.append(acc / l)
        o_ref[:, cols] = jnp.where(lane < HEAD_DIM, outs[0], outs[1])


def _moba_prompt(rel_bias, qkv, kmean, t5, b, t):
    tq = MOBA_BLOCK
    nq = t // tq
    nb = kmean.shape[1]
    return pl.pallas_call(
        _moba_body, grid=(b, nq),
        in_specs=[pl.BlockSpec(memory_space=pltpu.SMEM),
                  pl.BlockSpec((tq, D_ATT), lambda i, j: (i * nq + j, 0)),
                  pl.BlockSpec((t, D_ATT), lambda i, j: (i, 1)),
                  pl.BlockSpec((t, D_ATT), lambda i, j: (i, 2)),
                  pl.BlockSpec((None, nb, D_ATT), lambda i, j: (i, 0, 0)),
                  pl.BlockSpec(t5.shape, lambda i, j: (0, 0, 0, 0))],
        out_specs=pl.BlockSpec((tq, D_ATT), lambda i, j: (i * nq + j, 0)),
        out_shape=jax.ShapeDtypeStruct((b * t, D_ATT), f32),
        compiler_params=_cp(("parallel", "arbitrary")), name="moba_prompt")(rel_bias, qkv, qkv, qkv, kmean, t5)


def _outproj_body(x_ref, a_ref, f_ref, m_ref, w_ref, o_ref):
    acc = _dot(a_ref[...].astype(bf16), w_ref[0:D_CONV, :])
    acc = acc + _dot(f_ref[...].astype(bf16), w_ref[D_CONV:D_CONV + D_ATT, :])
    acc = acc + _dot(m_ref[...].astype(bf16), w_ref[D_CONV + D_ATT:, :])
    o_ref[...] = x_ref[...] + acc


def _outproj(x, a, fo, mo, w):
    m = x.shape[0]
    tm = min(ROW_TILE, m)
    row = lambda wd: pl.BlockSpec((tm, wd), lambda i: (i, 0))
    return pl.pallas_call(
        _outproj_body, grid=(m // tm,),
        in_specs=[row(D_MODEL), row(D_CONV), row(D_ATT), row(D_ATT), pl.BlockSpec(w.shape, lambda i: (0, 0))],
        out_specs=row(D_MODEL), out_shape=jax.ShapeDtypeStruct(x.shape, f32),
        compiler_params=_cp(("parallel",)), name="outproj")(x, a, fo, mo, w)


def _router_body(x_ref, g_ref, w_ref, o_ref):
    logits = _dot(_rms(x_ref[...], g_ref[...]).astype(bf16), w_ref[...])
    lane = lax.broadcasted_iota(jnp.int32, logits.shape, 1)
    g = jnp.where(lane < N_EXPERTS, logits, -jnp.inf)
    v1 = jnp.max(g, axis=-1, keepdims=True)
    i1 = jnp.min(jnp.where(g == v1, lane, LANES), axis=-1, keepdims=True)
    g2 = jnp.where(lane == i1, -jnp.inf, g)
    v2 = jnp.max(g2, axis=-1, keepdims=True)
    i2 = jnp.min(jnp.where(g2 == v2, lane, LANES), axis=-1, keepdims=True)
    e2 = jnp.exp(v2 - v1)
    w1 = 1.0 / (1.0 + e2)
    w2 = e2 / (1.0 + e2)
    o_ref[...] = jnp.where(lane == i1, w1, 0.0) + jnp.where(lane == i2, w2, 0.0)


def _router(x, g, w_pad):
    m = x.shape[0]
    tm = min(ROW_TILE, m)
    return pl.pallas_call(
        _router_body, grid=(m // tm,),
        in_specs=[pl.BlockSpec((tm, D_MODEL), lambda i: (i, 0)), pl.BlockSpec(g.shape, lambda i: (0, 0)),
                  pl.BlockSpec(w_pad.shape, lambda i: (0, 0))],
        out_specs=pl.BlockSpec((tm, LANES), lambda i: (i, 0)),
        out_shape=jax.ShapeDtypeStruct((m, LANES), f32),
        compiler_params=_cp(("parallel",)), name="router")(x, g, w_pad)


def _ffn_body(has_comb, has_final, x_ref, g_ref, *rest):
    rest = list(rest)
    comb_ref = rest.pop(0) if has_comb else None
    wg_ref, wu_ref, wd_ref = rest[:3]
    rest = rest[3:]
    gfin_ref = rest.pop(0) if has_final else None
    o_ref, hn_s, acc_s = rest
    e, j = pl.program_id(1), pl.program_id(2)
    first = (e == 0) & (j == 0)
    last = (e == pl.num_programs(1) - 1) & (j == pl.num_programs(2) - 1)

    @pl.when(first)
    def _():
        hn_s[...] = _rms(x_ref[...], g_ref[...]).astype(bf16)
        acc_s[...] = jnp.zeros_like(acc_s)

    hn = hn_s[...]
    h = _silu(_dot(hn, wg_ref[...].astype(bf16))) * _dot(hn, wu_ref[...].astype(bf16))
    if has_comb:
        comb = comb_ref[...]
        lane = lax.broadcasted_iota(jnp.int32, comb.shape, 1)
        h = h * jnp.sum(jnp.where(lane == e, comb, 0.0), axis=-1, keepdims=True)
    acc_s[...] += _dot(h.astype(bf16), wd_ref[...].astype(bf16))

    @pl.when(last)
    def _():
        y = x_ref[...] + acc_s[...]
        o_ref[...] = _rms(y, gfin_ref[...]) if has_final else y


def _ffn(x, g, comb, wg, wu, wd, gfin):
    m = x.shape[0]
    n_e, _, n_f = wg.shape
    tm = min(FFN_ROW_TILE, m)
    tf = FFN_COL_TILE
    in_specs = [pl.BlockSpec((tm, D_MODEL), lambda i, e, j: (i, 0)), pl.BlockSpec(g.shape, lambda i, e, j: (0, 0))]
    args = [x, g]
    if comb is not None:
        in_specs.append(pl.BlockSpec((tm, LANES), lambda i, e, j: (i, 0)))
        args.append(comb)
    in_specs += [pl.BlockSpec((None, D_MODEL, tf), lambda i, e, j: (e, 0, j)),
                 pl.BlockSpec((None, D_MODEL, tf), lambda i, e, j: (e, 0, j)),
                 pl.BlockSpec((None, tf, D_MODEL), lambda i, e, j: (e, j, 0))]
    args += [wg, wu, wd]
    if gfin is not None:
        in_specs.append(pl.BlockSpec(gfin.shape, lambda i, e, j: (0, 0)))
        args.append(gfin)
    return pl.pallas_call(
        functools.partial(_ffn_body, comb is not None, gfin is not None),
        grid=(m // tm, n_e, n_f // tf),
        in_specs=in_specs,
        out_specs=pl.BlockSpec((tm, D_MODEL), lambda i, e, j: (i, 0)),
        out_shape=jax.ShapeDtypeStruct(x.shape, f32),
        scratch_shapes=[pltpu.VMEM((tm, D_MODEL), bf16), pltpu.VMEM((tm, D_MODEL), f32)],
        compiler_params=_cp(("parallel", "arbitrary", "arbitrary")), name="ffn")(*args)


def _page_spec(layer, slot, with_dim, chunk=lambda c: c):
    if with_dim:
        return pl.BlockSpec((None, None, PAGE_SIZE, N_HEADS, HEAD_DIM),
                            lambda b, c, pt: (layer, pt[b, chunk(c) * DEC_PAGES + slot], 0, 0, 0))
    return pl.BlockSpec((None, None, PAGE_SIZE, N_HEADS),
                        lambda b, c, pt: (layer, pt[b, chunk(c) * DEC_PAGES + slot], 0, 0))


def _logf_gather_body(pt_ref, *refs):
    o_ref = refs[-1]
    for s in range(DEC_PAGES):
        o_ref[s] = refs[s][...]


def _logf_gather(page_table, cache_logf, layer):
    b, n_pages = page_table.shape
    return pl.pallas_call(
        _logf_gather_body,
        grid_spec=pltpu.PrefetchScalarGridSpec(
            num_scalar_prefetch=1, grid=(b, n_pages // DEC_PAGES),
            in_specs=[_page_spec(layer, s, False) for s in range(DEC_PAGES)],
            out_specs=pl.BlockSpec((None, DEC_PAGES, PAGE_SIZE, N_HEADS), lambda i, c, pt: (i, c, 0, 0))),
        out_shape=jax.ShapeDtypeStruct((b, n_pages, PAGE_SIZE, N_HEADS), f32),
        compiler_params=_cp(("parallel", "arbitrary")), name="logf_gather")(page_table, *([cache_logf] * DEC_PAGES))


def _suffix_body(lf_ref, new_ref, o_ref):
    x = lf_ref[...]
    n = x.shape[1]
    pos = lax.broadcasted_iota(jnp.int32, x.shape, 1)
    y = x
    s = 1
    while s < n:
        y = y + jnp.where(pos + s < n, pltpu.roll(y, n - s, axis=1), 0.0)
        s *= 2
    o_ref[...] = (y - x) + new_ref[:, 0:1]


def _suffix_bias(lf_t, lf_new):
    b, _, n = lf_t.shape
    return pl.pallas_call(
        _suffix_body, grid=(b,),
        in_specs=[pl.BlockSpec((None, SUBLANES, n), lambda i: (i, 0, 0)),
                  pl.BlockSpec((None, SUBLANES, LANES), lambda i: (i, 0, 0))],
        out_specs=pl.BlockSpec((None, SUBLANES, n), lambda i: (i, 0, 0)),
        out_shape=jax.ShapeDtypeStruct(lf_t.shape, f32),
        compiler_params=_cp(("parallel",)), name="suffix_bias")(lf_t, lf_new)


def _head_rows(x8):
    row = lax.broadcasted_iota(jnp.int32, (SUBLANES, HEAD_DIM), 0)
    return [jnp.where(row == h, x8, 0.0).astype(bf16) for h in range(N_HEADS)]


def _page_scores(q_rows, k_page):
    kt = pltpu.einshape("phd->hpd", k_page).astype(bf16)
    s = _dot_nt(q_rows[0], kt[0])
    for h in range(1, N_HEADS):
        s = s + _dot_nt(q_rows[h], kt[h])
    return s


def _page_values(p, v_page):
    vt = pltpu.einshape("phd->hpd", v_page).astype(bf16)
    row = lax.broadcasted_iota(jnp.int32, p.shape, 0)
    pb = p.astype(bf16)
    o = jnp.zeros((SUBLANES, HEAD_DIM), f32)
    for h in range(N_HEADS):
        o = o + _dot(jnp.where(row == h, pb, jnp.zeros_like(pb)), vt[h])
    return o


def _fox_dec_body(pt_ref, q_ref, kn_ref, vn_ref, bias_ref, *refs):
    k_refs, v_refs = refs[:DEC_PAGES], refs[DEC_PAGES:2 * DEC_PAGES]
    o_ref, s_all, m_s, inv_s, acc_s = refs[2 * DEC_PAGES:]
    nc = s_all.shape[0]
    c = pl.program_id(1)
    q = q_ref[...]

    @pl.when(c == 0)
    def _():
        m_s[...] = jnp.full_like(m_s, -jnp.inf)

    @pl.when(c < nc)
    def _():
        q_rows = _head_rows(q)
        s = jnp.concatenate([_page_scores(q_rows, k_refs[g][...]) for g in range(DEC_PAGES)], axis=1) + bias_ref[...]
        s_all[c] = s
        m_s[...] = jnp.maximum(m_s[...], jnp.max(s, axis=-1, keepdims=True))

    @pl.when(c == nc)
    def _():
        s_new = jnp.sum(_round_bf16(q) * _round_bf16(kn_ref[...]), axis=-1, keepdims=True)
        m = jnp.maximum(m_s[...], s_new)
        p_new = jnp.exp(s_new - m)
        l = p_new
        for n in range(nc):
            p = jnp.exp(s_all[n] - m)
            s_all[n] = p
            l = l + jnp.sum(p, axis=-1, keepdims=True)
        inv = 1.0 / l
        inv_s[...] = inv
        acc_s[...] = _round_bf16(p_new * inv) * _round_bf16(vn_ref[...])

    @pl.when(c >= nc)
    def _():
        p = s_all[c - nc] * inv_s[...]
        pv = acc_s[...]
        for g in range(DEC_PAGES):
            pv = pv + _page_values(p[:, g * PAGE_SIZE:(g + 1) * PAGE_SIZE], v_refs[g][...])
        acc_s[...] = pv

    @pl.when(c == 2 * nc - 1)
    def _():
        o_ref[...] = acc_s[...]


def _fox_decode(page_table, q, k_new, v_new, bias, cache_k, cache_v, layer):
    b, n_pages = page_table.shape
    nc = n_pages // DEC_PAGES
    small = pl.BlockSpec((None, SUBLANES, HEAD_DIM), lambda i, c, pt: (i, 0, 0))
    k_chunk = lambda c: jnp.minimum(c, nc - 1)
    v_chunk = lambda c: jnp.maximum(c - nc, 0)
    return pl.pallas_call(
        _fox_dec_body,
        grid_spec=pltpu.PrefetchScalarGridSpec(
            num_scalar_prefetch=1, grid=(b, 2 * nc),
            in_specs=[small, small, small,
                      pl.BlockSpec((None, SUBLANES, DEC_PAGES * PAGE_SIZE), lambda i, c, pt: (i, 0, k_chunk(c)))]
            + [_page_spec(layer, s, True, k_chunk) for s in range(DEC_PAGES)]
            + [_page_spec(layer, s, True, v_chunk) for s in range(DEC_PAGES)],
            out_specs=pl.BlockSpec((None, SUBLANES, HEAD_DIM), lambda i, c, pt: (i, 0, 0)),
            scratch_shapes=[pltpu.VMEM((nc, SUBLANES, DEC_PAGES * PAGE_SIZE), f32), pltpu.VMEM((SUBLANES, 1), f32),
                            pltpu.VMEM((SUBLANES, 1), f32), pltpu.VMEM((SUBLANES, HEAD_DIM), f32)]),
        out_shape=jax.ShapeDtypeStruct((b, SUBLANES, HEAD_DIM), f32),
        compiler_params=_cp(("parallel", "arbitrary")), name="fox_decode")(
            page_table, q, k_new, v_new, bias, *([cache_k] * DEC_PAGES), *([cache_v] * DEC_PAGES))


def _block_sum_body(pt_ref, *refs):
    o_ref = refs[-1]
    pages_per_block = MOBA_BLOCK // PAGE_SIZE
    for blk in range(DEC_PAGES // pages_per_block):
        acc = jnp.sum(refs[blk * pages_per_block][...], axis=0)
        for s in range(1, pages_per_block):
            acc = acc + jnp.sum(refs[blk * pages_per_block + s][...], axis=0)
        o_ref[blk] = acc * (1.0 / MOBA_BLOCK)


def _kmean_decode(page_table, cache_k, layer):
    b, n_pages = page_table.shape
    per_step = DEC_PAGES * PAGE_SIZE // MOBA_BLOCK
    return pl.pallas_call(
        _block_sum_body,
        grid_spec=pltpu.PrefetchScalarGridSpec(
            num_scalar_prefetch=1, grid=(b, n_pages // DEC_PAGES),
            in_specs=[_page_spec(layer, s, True) for s in range(DEC_PAGES)],
            out_specs=pl.BlockSpec((None, per_step, N_HEADS, HEAD_DIM), lambda i, c, pt: (i, c, 0, 0))),
        out_shape=jax.ShapeDtypeStruct((b, n_pages * PAGE_SIZE // MOBA_BLOCK, N_HEADS, HEAD_DIM), f32),
        compiler_params=_cp(("parallel", "arbitrary")), name="kmean_decode")(page_table, *([cache_k] * DEC_PAGES))


def _select_body(q_ref, km_ref, o_ref):
    km = km_ref[...]
    nb = km.shape[0]
    g = jnp.sum(_round_bf16(km) * _round_bf16(q_ref[...])[None], axis=-1, keepdims=True)
    blk = lax.broadcasted_iota(jnp.int32, g.shape, 0)
    for t in range(MOBA_TOPK):
        mx = jnp.max(g, axis=0, keepdims=True)
        idx = jnp.min(jnp.where(g == mx, blk, nb), axis=0, keepdims=True)
        o_ref[t] = idx[0]
        g = jnp.where(blk == idx, -jnp.inf, g)


def _select_decode(q, kmean):
    b, nb = kmean.shape[:2]
    return pl.pallas_call(
        _select_body, grid=(b,),
        in_specs=[pl.BlockSpec((None, N_HEADS, HEAD_DIM), lambda i: (i, 0, 0)),
                  pl.BlockSpec((None, nb, N_HEADS, HEAD_DIM), lambda i: (i, 0, 0, 0))],
        out_specs=pl.BlockSpec((None, MOBA_TOPK, N_HEADS, 1), lambda i: (i, 0, 0, 0)),
        out_shape=jax.ShapeDtypeStruct((b, MOBA_TOPK, N_HEADS, 1), jnp.int32),
        compiler_params=_cp(("parallel",)), name="select_decode")(q, kmean)


def _moba_dec_body(past_len, pt_ref, idx_ref, q_ref, kn_ref, vn_ref, tab_ref, *refs):
    pages_per_block = MOBA_BLOCK // PAGE_SIZE
    n_sel = MOBA_TOPK * pages_per_block
    k_refs, v_refs, o_ref = refs[:n_sel], refs[n_sel:2 * n_sel], refs[2 * n_sel]
    b, h = pl.program_id(0), pl.program_id(1)
    q = q_ref[...]
    q_rows = _head_rows(q)
    tab = tab_ref[...]
    lane = lax.broadcasted_iota(jnp.int32, (1, PAGE_SIZE), 1)

    def t5(dist):
        bucket = _t5_bucket(dist)
        out = jnp.zeros((SUBLANES, dist.shape[1]), f32)
        for nbk in range(N_BUCKETS):
            out = jnp.where(bucket == nbk, tab[:, nbk:nbk + 1], out)
        return out

    scores = []
    for t in range(MOBA_TOPK):
        blk = idx_ref[b, t, h]
        for s in range(pages_per_block):
            kpos = blk * MOBA_BLOCK + s * PAGE_SIZE + lane
            scores.append(_page_scores(q_rows, k_refs[t * pages_per_block + s][...]) + t5(past_len - kpos))
    s_past = jnp.concatenate(scores, axis=1)
    s_new = (jnp.sum(_round_bf16(q) * _round_bf16(kn_ref[...]), axis=-1, keepdims=True)
             + t5(jnp.zeros((1, 1), jnp.int32)))
    m = jnp.maximum(jnp.max(s_past, axis=-1, keepdims=True), s_new)
    p = jnp.exp(s_past - m)
    p_new = jnp.exp(s_new - m)
    inv = 1.0 / (jnp.sum(p, axis=-1, keepdims=True) + p_new)
    p = p * inv
    acc = _round_bf16(p_new * inv) * _round_bf16(vn_ref[...])
    for g in range(n_sel):
        acc = acc + _page_values(p[:, g * PAGE_SIZE:(g + 1) * PAGE_SIZE], v_refs[g][...])
    row = lax.broadcasted_iota(jnp.int32, (SUBLANES, HEAD_DIM), 0)
    res = jnp.where(row == h, acc, 0.0)

    @pl.when(h == 0)
    def _():
        o_ref[...] = res

    @pl.when(h > 0)
    def _():
        o_ref[...] += res


def _moba_decode(page_table, idx, q, k_new, v_new, tab_t, cache_k, cache_v, layer):
    b, n_pages = page_table.shape
    pages_per_block = MOBA_BLOCK // PAGE_SIZE
    small = pl.BlockSpec((None, SUBLANES, HEAD_DIM), lambda i, h, pt, ix: (i, 0, 0))

    def sel_spec(t, s):
        return pl.BlockSpec((None, None, PAGE_SIZE, N_HEADS, HEAD_DIM),
                            lambda i, h, pt, ix: (layer, pt[i, ix[i, t, h] * pages_per_block + s], 0, 0, 0))

    specs = [sel_spec(t, s) for t in range(MOBA_TOPK) for s in range(pages_per_block)]
    return pl.pallas_call(
        functools.partial(_moba_dec_body, n_pages * PAGE_SIZE),
        grid_spec=pltpu.PrefetchScalarGridSpec(
            num_scalar_prefetch=2, grid=(b, N_HEADS),
            in_specs=[small, small, small, pl.BlockSpec(tab_t.shape, lambda i, h, pt, ix: (0, 0))] + specs * 2,
            out_specs=pl.BlockSpec((None, SUBLANES, HEAD_DIM), lambda i, h, pt, ix: (i, 0, 0))),
        out_shape=jax.ShapeDtypeStruct((b, SUBLANES, HEAD_DIM), f32),
        compiler_params=_cp(("parallel", "arbitrary")), name="moba_decode")(
            page_table, idx, q, k_new, v_new, tab_t, *([cache_k] * len(specs)), *([cache_v] * len(specs)))


def _split_w_in(w):
    o_f = 2 * D_CONV
    o_g = o_f + 3 * D_ATT
    o_m = o_g + N_HEADS
    wg = jnp.zeros((D_MODEL, LANES), f32).at[:, :N_HEADS].set(w[:, o_g:o_m])
    return w[:, :o_f].astype(bf16), w[:, o_f:o_g].astype(bf16), w[:, o_m:].astype(bf16), wg.astype(bf16)


def _pad_lanes(v):
    return jnp.zeros((1, LANES), f32).at[0, :v.shape[0]].set(v)


def _layer_weights(l, p):
    wglu, wf, wm, wg = _split_w_in(p["w_in"][l])
    i = l // 2
    d = dict(norm_mix=p["norm_mix"][l][None], norm_ffn=p["norm_ffn"][l][None],
             wglu=wglu, wf=wf, wm=wm, wg=wg, bfg=_pad_lanes(p["b_forget"][l]),
             conv_w=p["conv_w"][l], conv_b=p["conv_b"][l][None], ln_g=p["conv_ln_g"][l][None], ln_b=p["conv_ln_b"][l][None],
             w_pw=p["w_conv_pw"][l].astype(bf16), w_out=p["w_out"][l].astype(bf16))
    if l % 2 == 0:
        d.update(ffn_g=p["w_dense_gate"][i][None], ffn_u=p["w_dense_up"][i][None], ffn_d=p["w_dense_down"][i][None], router=None)
    else:
        d.update(ffn_g=p["w_moe_gate"][i], ffn_u=p["w_moe_up"][i], ffn_d=p["w_moe_down"][i],
                 router=jnp.zeros((D_MODEL, LANES), f32).at[:, :N_EXPERTS].set(p["w_router"][i]).astype(bf16))
    return d


def _channel_mix(x, w, gfin):
    comb = None if w["router"] is None else _router(x, w["norm_ffn"], w["router"])
    return _ffn(x, w["norm_ffn"], comb, w["ffn_g"], w["ffn_u"], w["ffn_d"], gfin)


def _heads(a, b, t):
    return a.reshape(b, t, N_HEADS, HEAD_DIM)


def _prompt_layer(x, w, rel_bias, t5, b, t, gfin):
    u, ff, fm, bff, bfm, lf, lft = _proj(x, w["norm_mix"], w["wglu"], w["wf"], w["wm"], w["wg"], w["bfg"], True)
    u3 = u.reshape(b, t, D_CONV)
    up = jnp.concatenate([jnp.zeros((b, CONV_HALO, D_CONV), f32), u3], axis=1)
    conv_out = _conv_prompt(up, w["conv_w"], w["conv_b"], w["ln_g"], w["ln_b"], w["w_pw"], t).reshape(b * t, D_CONV)
    ccol, crow = _cumsum(lf, lft, b, t)
    nq = t // min(ATT_TILE, t)
    crow3 = crow.reshape(SUBLANES, b * nq, t // nq).transpose(1, 0, 2)
    fox_out = _fox_prompt(bff, ccol, crow3, b, t)
    kmean = _kmean_prompt(fm, b, t)
    moba_out = _moba_prompt(rel_bias, bfm, kmean, t5, b, t)
    x = _outproj(x, conv_out, fox_out, moba_out, w["w_out"])
    x = _channel_mix(x, w, gfin)
    state = (_heads(ff[:, D_ATT:2 * D_ATT], b, t), _heads(ff[:, 2 * D_ATT:], b, t), lf[:, :N_HEADS].reshape(b, t, N_HEADS),
             _heads(fm[:, D_ATT:2 * D_ATT], b, t), _heads(fm[:, 2 * D_ATT:], b, t), u3[:, t - (CONV_WIDTH - 1):])
    return x, state


def _sample_layer(x, w, l, rel_bias, caches, page_table, gfin):
    cache_fk, cache_fv, cache_flf, cache_mk, cache_mv, state_conv = caches
    b = x.shape[0]
    n_past = page_table.shape[1] * PAGE_SIZE
    u, ff, fm, _, _, lf = _proj(x, w["norm_mix"], w["wglu"], w["wf"], w["wm"], w["wg"], w["bfg"], False)
    conv_out = _conv_step(state_conv[l].transpose(1, 0, 2), u, w["conv_w"], w["conv_b"], w["ln_g"], w["ln_b"], w["w_pw"])
    heads = lambda a: a.reshape(b, N_HEADS, HEAD_DIM)
    fq, fk, fv = heads(ff[:, :D_ATT]) * Q_SCALE, heads(ff[:, D_ATT:2 * D_ATT]), heads(ff[:, 2 * D_ATT:])
    mq, mk, mv = heads(fm[:, :D_ATT]) * Q_SCALE, heads(fm[:, D_ATT:2 * D_ATT]), heads(fm[:, 2 * D_ATT:])
    lf_pages = _logf_gather(page_table, cache_flf, l)
    lf_t = jnp.pad(lf_pages.reshape(b, n_past, N_HEADS).transpose(0, 2, 1), ((0, 0), (0, SUBLANES - N_HEADS), (0, 0)))
    lf_new = jnp.broadcast_to(jnp.pad(lf[:, :N_HEADS], ((0, 0), (0, SUBLANES - N_HEADS)))[:, :, None], (b, SUBLANES, LANES))
    bias = _suffix_bias(lf_t, lf_new)
    pad8 = lambda a: jnp.pad(a, ((0, 0), (0, SUBLANES - N_HEADS), (0, 0)))
    fox_out = _fox_decode(page_table, pad8(fq), pad8(fk), pad8(fv), bias, cache_fk, cache_fv, l)[:, :N_HEADS].reshape(b, D_ATT)
    kmean = _kmean_decode(page_table, cache_mk, l)
    idx = _select_decode(mq, kmean)[..., 0]
    tab_t = jnp.pad(rel_bias.T, ((0, SUBLANES - N_HEADS), (0, 0)))
    moba_out = _moba_decode(page_table, idx, pad8(mq), pad8(mk), pad8(mv), tab_t, cache_mk, cache_mv, l)[:, :N_HEADS].reshape(b, D_ATT)
    x = _outproj(x, conv_out, fox_out, moba_out, w["w_out"])
    x = _channel_mix(x, w, gfin)
    conv_state = jnp.concatenate([state_conv[l][:, 1:], u[:, None]], axis=1)
    state = (fk[:, None], fv[:, None], lf[:, None, :N_HEADS], mk[:, None], mv[:, None], conv_state)
    return x, state


def kernel(x_prompt, x_sample, cache_fox_k, cache_fox_v, cache_fox_logf, cache_moba_k, cache_moba_v, state_conv, page_table, norm_mix, norm_ffn, norm_final, w_in, b_forget, conv_w, conv_b, conv_ln_g, conv_ln_b, w_conv_pw, rel_bias, w_out, w_dense_gate, w_dense_up, w_dense_down, w_router, w_moe_gate, w_moe_up, w_moe_down):
    p = dict(norm_mix=norm_mix, norm_ffn=norm_ffn, w_in=w_in, b_forget=b_forget, conv_w=conv_w, conv_b=conv_b,
             conv_ln_g=conv_ln_g, conv_ln_b=conv_ln_b, w_conv_pw=w_conv_pw, w_out=w_out, w_dense_gate=w_dense_gate,
             w_dense_up=w_dense_up, w_dense_down=w_dense_down, w_router=w_router, w_moe_gate=w_moe_gate,
             w_moe_up=w_moe_up, w_moe_down=w_moe_down)
    depth = norm_mix.shape[0]
    weights = [_layer_weights(l, p) for l in range(depth)]
    gfin = norm_final[None]
    t5 = _t5_tiles(rel_bias)

    b, t, _ = x_prompt.shape
    x = x_prompt.reshape(b * t, D_MODEL)
    states_p = []
    for l in range(depth):
        x, st = _prompt_layer(x, weights[l], rel_bias, t5, b, t, gfin if l == depth - 1 else None)
        states_p.append(st)
    y_prompt = x.reshape(b, t, D_MODEL)

    bs = x_sample.shape[0]
    caches = (cache_fox_k, cache_fox_v, cache_fox_logf, cache_moba_k, cache_moba_v, state_conv)
    x = x_sample.reshape(bs, D_MODEL)
    states_s = []
    for l in range(depth):
        x, st = _sample_layer(x, weights[l], l, rel_bias, caches, page_table, gfin if l == depth - 1 else None)
        states_s.append(st)
    y_sample = x.reshape(bs, 1, D_MODEL)

    stack = lambda sts: [jnp.stack(s) for s in zip(*sts)]
    return (y_prompt, y_sample, *stack(states_p), *stack(states_s))
```

```python
import functools
import math

import jax
import jax.numpy as jnp
from jax import lax
from jax.experimental import pallas as pl
from jax.experimental.pallas import tpu as pltpu

f32 = jnp.float32
bf16 = jnp.bfloat16

D_MODEL = 1024
HEAD_DIM = 64
N_HEADS = 6
D_ATT = N_HEADS * HEAD_DIM
D_CONV = 256
CONV_WIDTH = 31
PAGE_SIZE = 128
MOBA_BLOCK = 256
MOBA_TOPK = 3
N_BUCKETS = 32
MAX_DISTANCE = 128
N_EXPERTS = 8
Q_SCALE = HEAD_DIM ** -0.5

LANES = 128
SUBLANES = 8
VMEM_LIMIT = 56 * 1024 * 1024

ROW_TILE = 256
ATT_TILE = 256
CONV_TILE = 512
CONV_HALO = 32
FFN_ROW_TILE = 1024
FFN_COL_TILE = 256
DEC_PAGES = 16


def _cp(sem):
    return pltpu.CompilerParams(dimension_semantics=sem, vmem_limit_bytes=VMEM_LIMIT)


def _rms(x, g):
    return (x * lax.rsqrt(jnp.mean(x * x, axis=-1, keepdims=True) + 1e-6)) * g


def _log_sigmoid(x):
    return jnp.minimum(x, 0.0) - jnp.log(1.0 + jnp.exp(-jnp.abs(x)))


def _silu(x):
    return x * jax.nn.sigmoid(x)


def _round_bf16(x):
    return x.astype(bf16).astype(f32)


def _dot(a, b):
    return jnp.dot(a, b, preferred_element_type=f32)


def _dot_nt(a, b, precision=None):
    return lax.dot_general(a, b, (((1,), (1,)), ((), ())), preferred_element_type=f32, precision=precision)


def _t5_bucket(d):
    max_exact = N_BUCKETS // 2
    df = jnp.maximum(d, max_exact).astype(f32)
    large = max_exact + (jnp.log(df / max_exact) / math.log(MAX_DISTANCE / max_exact)
                         * (N_BUCKETS - max_exact)).astype(jnp.int32)
    return jnp.where(d < max_exact, d, jnp.minimum(large, N_BUCKETS - 1))


def _proj_body(with_t, x_ref, g_ref, wglu_ref, wf_ref, wm_ref, wg_ref, bfg_ref,
               u_ref, ff_ref, fm_ref, bff_ref, bfm_ref, lf_ref, *maybe_lft):
    hb = _rms(x_ref[...], g_ref[...]).astype(bf16)
    glu = _dot(hb, wglu_ref[...])
    u_ref[...] = glu[:, :D_CONV] * jax.nn.sigmoid(glu[:, D_CONV:])
    col = lax.broadcasted_iota(jnp.int32, (1, 3 * D_ATT), 1)
    qscale = jnp.where(col < D_ATT, Q_SCALE, 1.0).astype(f32)
    pf = _dot(hb, wf_ref[...])
    ff_ref[...] = pf
    bff_ref[...] = (pf * qscale).astype(bf16)
    pm = _dot(hb, wm_ref[...])
    fm_ref[...] = pm
    bfm_ref[...] = (pm * qscale).astype(bf16)
    lf = _log_sigmoid(_dot(hb, wg_ref[...]) + bfg_ref[...])
    lf_ref[...] = lf
    if with_t:
        maybe_lft[0][...] = lf.T[:SUBLANES]


def _proj(x, g, wglu, wf, wm, wg, bfg, with_t):
    m = x.shape[0]
    tm = min(ROW_TILE, m)
    row = lambda w: pl.BlockSpec((tm, w), lambda i: (i, 0))
    full = lambda a: pl.BlockSpec(a.shape, lambda i: (0, 0))
    out_shape = [jax.ShapeDtypeStruct((m, D_CONV), f32),
                 jax.ShapeDtypeStruct((m, 3 * D_ATT), f32), jax.ShapeDtypeStruct((m, 3 * D_ATT), f32),
                 jax.ShapeDtypeStruct((m, 3 * D_ATT), bf16), jax.ShapeDtypeStruct((m, 3 * D_ATT), bf16),
                 jax.ShapeDtypeStruct((m, LANES), f32)]
    out_specs = [row(D_CONV), row(3 * D_ATT), row(3 * D_ATT), row(3 * D_ATT), row(3 * D_ATT), row(LANES)]
    if with_t:
        out_shape.append(jax.ShapeDtypeStruct((SUBLANES, m), f32))
        out_specs.append(pl.BlockSpec((SUBLANES, tm), lambda i: (0, i)))
    return pl.pallas_call(
        functools.partial(_proj_body, with_t),
        grid=(m // tm,),
        in_specs=[row(D_MODEL), full(g), full(wglu), full(wf), full(wm), full(wg), full(bfg)],
        out_specs=out_specs, out_shape=out_shape,
        compiler_params=_cp(("parallel",)), name="proj")(x, g, wglu, wf, wm, wg, bfg)


def _cumsum_body(lf_ref, lft_ref, ccol_ref, crow_ref):
    x = lf_ref[...]
    t = x.shape[0]
    pos = lax.broadcasted_iota(jnp.int32, x.shape, 0)
    s = 1
    while s < t:
        x = x + jnp.where(pos >= s, pltpu.roll(x, s, axis=0), 0.0)
        s *= 2
    ccol_ref[...] = x
    y = lft_ref[...]
    pos = lax.broadcasted_iota(jnp.int32, y.shape, 1)
    s = 1
    while s < t:
        y = y + jnp.where(pos >= s, pltpu.roll(y, s, axis=1), 0.0)
        s *= 2
    crow_ref[...] = y


def _cumsum(lf, lft, b, t):
    return pl.pallas_call(
        _cumsum_body, grid=(b,),
        in_specs=[pl.BlockSpec((t, LANES), lambda i: (i, 0)), pl.BlockSpec((SUBLANES, t), lambda i: (0, i))],
        out_specs=[pl.BlockSpec((t, LANES), lambda i: (i, 0)), pl.BlockSpec((SUBLANES, t), lambda i: (0, i))],
        out_shape=[jax.ShapeDtypeStruct(lf.shape, f32), jax.ShapeDtypeStruct(lft.shape, f32)],
        compiler_params=_cp(("parallel",)), name="cumsum")(lf, lft)


def _conv_tail(y, lng_ref, lnb_ref, wpw_ref):
    mu = jnp.mean(y, axis=-1, keepdims=True)
    var = jnp.mean(jnp.square(y - mu), axis=-1, keepdims=True)
    z = _silu((y - mu) * lax.rsqrt(var + 1e-5) * lng_ref[...] + lnb_ref[...])
    return _dot(z.astype(bf16), wpw_ref[...])


def _conv_body(main_ref, halo_ref, w_ref, b_ref, lng_ref, lnb_ref, wpw_ref, o_ref):
    tt = main_ref.shape[0]
    win = _round_bf16(jnp.concatenate([main_ref[...], halo_ref[...]], axis=0))
    w = _round_bf16(w_ref[...])
    lead = CONV_HALO - (CONV_WIDTH - 1)
    acc = jnp.zeros((tt, D_CONV), f32) + b_ref[...]
    for k in range(CONV_WIDTH):
        acc = acc + win[lead + k:lead + k + tt] * w[k:k + 1]
    o_ref[...] = _conv_tail(acc, lng_ref, lnb_ref, wpw_ref)


def _conv_prompt(up, w, b, lng, lnb, wpw, t):
    bsz = up.shape[0]
    tt = min(CONV_TILE, t)
    full = lambda a: pl.BlockSpec(a.shape, lambda i, j: (0, 0))
    return pl.pallas_call(
        _conv_body, grid=(bsz, t // tt),
        in_specs=[pl.BlockSpec((None, tt, D_CONV), lambda i, j: (i, j, 0)),
                  pl.BlockSpec((None, CONV_HALO, D_CONV), lambda i, j: (i, (j + 1) * (tt // CONV_HALO), 0)),
                  full(w), full(b), full(lng), full(lnb), full(wpw)],
        out_specs=pl.BlockSpec((None, tt, D_CONV), lambda i, j: (i, j, 0)),
        out_shape=jax.ShapeDtypeStruct((bsz, t, D_CONV), f32),
        compiler_params=_cp(("parallel", "parallel")), name="conv_prompt")(up, up, w, b, lng, lnb, wpw)


def _conv_step_body(st_ref, u_ref, w_ref, b_ref, lng_ref, lnb_ref, wpw_ref, o_ref):
    w = w_ref[...]
    acc = u_ref[...] * w[CONV_WIDTH - 1:CONV_WIDTH] + b_ref[...]
    for k in range(CONV_WIDTH - 1):
        acc = acc + st_ref[k] * w[k:k + 1]
    o_ref[...] = _conv_tail(acc, lng_ref, lnb_ref, wpw_ref)


def _conv_step(state_t, u, w, b, lng, lnb, wpw):
    return pl.pallas_call(
        _conv_step_body, out_shape=jax.ShapeDtypeStruct(u.shape, f32), name="conv_step")(state_t, u, w, b, lng, lnb, wpw)


def _flash_step(qh, k, v, bias, mask, carry):
    m, l, acc = carry
    s = _dot_nt(qh, k) + bias
    if mask is not None:
        s = jnp.where(mask, s, -jnp.inf)
    m_new = jnp.maximum(m, jnp.max(s, axis=-1, keepdims=True))
    alpha = jnp.exp(m - m_new)
    p = jnp.exp(s - m_new)
    l = alpha * l + jnp.sum(p, axis=-1, keepdims=True)
    acc = alpha * acc + _dot(p.astype(bf16), v)
    return m_new, l, acc


def _flash_init(tq):
    return (jnp.full((tq, 1), -jnp.inf, f32), jnp.zeros((tq, 1), f32), jnp.zeros((tq, LANES), f32))


def _fox_body(q_ref, k_ref, v_ref, ccol_ref, crow_ref, o_ref):
    tq = q_ref.shape[0]
    i = pl.program_id(1)
    lane = lax.broadcasted_iota(jnp.int32, (1, LANES), 1)
    r = lax.broadcasted_iota(jnp.int32, (tq, tq), 0)
    c = lax.broadcasted_iota(jnp.int32, (tq, tq), 1)
    causal = c <= r
    for j in range(N_HEADS // 2):
        cols = slice(j * LANES, (j + 1) * LANES)
        qp = q_ref[:, cols]
        outs = []
        for hh in range(2):
            h = 2 * j + hh
            in_head = (lane >= hh * HEAD_DIM) & (lane < (hh + 1) * HEAD_DIM)
            qh = jnp.where(in_head, qp, jnp.zeros_like(qp))
            cq = ccol_ref[:, h:h + 1]

            def tile(n, carry, mask, cols=cols, h=h, qh=qh, cq=cq):
                rows = pl.ds(pl.multiple_of(n * tq, tq), tq)
                ck = crow_ref[n][h:h + 1, :]
                return _flash_step(qh, k_ref[rows, cols], v_ref[rows, cols], cq - ck, mask, carry)

            carry = lax.fori_loop(0, i, lambda n, cr, tile=tile: tile(n, cr, None), _flash_init(tq))
            m, l, acc = tile(i, carry, causal)
            outs.append(acc / l)
        o_ref[:, cols] = jnp.where(lane < HEAD_DIM, outs[0], outs[1])


def _fox_prompt(qkv, ccol, crow3, b, t):
    tq = min(ATT_TILE, t)
    nq = t // tq
    return pl.pallas_call(
        _fox_body, grid=(b, nq),
        in_specs=[pl.BlockSpec((tq, D_ATT), lambda i, j: (i * nq + j, 0)),
                  pl.BlockSpec((t, D_ATT), lambda i, j: (i, 1)),
                  pl.BlockSpec((t, D_ATT), lambda i, j: (i, 2)),
                  pl.BlockSpec((tq, LANES), lambda i, j: (i * nq + j, 0)),
                  pl.BlockSpec((nq, SUBLANES, tq), lambda i, j: (i, 0, 0))],
        out_specs=pl.BlockSpec((tq, D_ATT), lambda i, j: (i * nq + j, 0)),
        out_shape=jax.ShapeDtypeStruct((b * t, D_ATT), f32),
        compiler_params=_cp(("parallel", "arbitrary")), name="fox_prompt")(qkv, qkv, qkv, ccol, crow3)


def _kmean_body(k_ref, o_ref):
    t = k_ref.shape[0]
    k = k_ref[...].reshape(t // MOBA_BLOCK, MOBA_BLOCK, D_ATT)
    o_ref[...] = jnp.mean(k, axis=1)


def _kmean_prompt(mqkv_f32, b, t):
    nb = t // MOBA_BLOCK
    return pl.pallas_call(
        _kmean_body, grid=(b,),
        in_specs=[pl.BlockSpec((t, D_ATT), lambda i: (i, 1))],
        out_specs=pl.BlockSpec((None, nb, D_ATT), lambda i: (i, 0, 0)),
        out_shape=jax.ShapeDtypeStruct((b, nb, D_ATT), f32),
        compiler_params=_cp(("parallel",)), name="kmean_prompt")(mqkv_f32)


def _t5_tiles_body(tab_ref, o_ref):
    h = pl.program_id(0)
    r = lax.broadcasted_iota(jnp.int32, (MOBA_BLOCK, MOBA_BLOCK), 0)
    c = lax.broadcasted_iota(jnp.int32, (MOBA_BLOCK, MOBA_BLOCK), 1)
    for which in range(2):
        bucket = _t5_bucket(jnp.maximum(r - c + which * MOBA_BLOCK, 0))
        acc = jnp.zeros((MOBA_BLOCK, MOBA_BLOCK), f32)
        for nbk in range(N_BUCKETS):
            acc = jnp.where(bucket == nbk, tab_ref[nbk, h], acc)
        o_ref[which] = acc


def _t5_tiles(rel_bias):
    return pl.pallas_call(
        _t5_tiles_body, grid=(N_HEADS,),
        in_specs=[pl.BlockSpec(memory_space=pltpu.SMEM)],
        out_specs=pl.BlockSpec((None, 2, MOBA_BLOCK, MOBA_BLOCK), lambda h: (h, 0, 0, 0)),
        out_shape=jax.ShapeDtypeStruct((N_HEADS, 2, MOBA_BLOCK, MOBA_BLOCK), f32),
        compiler_params=_cp(("parallel",)), name="t5_tiles")(rel_bias)


def _top_blocks(gate, n_valid):
    blk = lax.broadcasted_iota(jnp.int32, gate.shape, 1)
    nb = gate.shape[1]
    g = jnp.where(blk < n_valid, gate, -jnp.inf)
    sel = jnp.zeros(gate.shape, f32)
    for _ in range(min(MOBA_TOPK, nb)):
        mx = jnp.max(g, axis=-1, keepdims=True)
        idx = jnp.min(jnp.where(g == mx, blk, nb), axis=-1, keepdims=True)
        pick = (blk == idx) & (mx > -jnp.inf)
        sel = jnp.where(pick, 1.0, sel)
        g = jnp.where(blk == idx, -jnp.inf, g)
    return sel


def _moba_body(tab_ref, q_ref, k_ref, v_ref, kmean_ref, t5_ref, o_ref):
    tq = q_ref.shape[0]
    nb = kmean_ref.shape[0]
    i = pl.program_id(1)
    lane = lax.broadcasted_iota(jnp.int32, (1, LANES), 1)
    r = lax.broadcasted_iota(jnp.int32, (tq, tq), 0)
    c = lax.broadcasted_iota(jnp.int32, (tq, tq), 1)
    causal = c <= r
    blk = lax.broadcasted_iota(jnp.int32, (tq, nb), 1)
    prev = jnp.maximum(i - 1, 0)
    for j in range(N_HEADS // 2):
        cols = slice(j * LANES, (j + 1) * LANES)
        qp = q_ref[:, cols]
        kmp = kmean_ref[:, cols].astype(bf16)
        outs = []
        for hh in range(2):
            h = 2 * j + hh
            in_head = (lane >= hh * HEAD_DIM) & (lane < (hh + 1) * HEAD_DIM)
            qh = jnp.where(in_head, qp, jnp.zeros_like(qp))
            sel = _top_blocks(_dot_nt(qh, kmp), i)

            def picked(n, sel=sel):
                return jnp.max(jnp.where(blk == n, sel, 0.0), axis=-1, keepdims=True) > 0.5

            def tile(n, bias, mask, carry, cols=cols, qh=qh):
                rows = pl.ds(pl.multiple_of(n * tq, tq), tq)
                return _flash_step(qh, k_ref[rows, cols], v_ref[rows, cols], bias, mask, carry)

            carry = tile(i, t5_ref[h, 0], causal, _flash_init(tq))
            carry = tile(prev, t5_ref[h, 1], picked(prev), carry)
            far = tab_ref[N_BUCKETS - 1, h]
            carry = lax.fori_loop(0, prev, lambda n, cr, tile=tile, picked=picked, far=far: tile(n, far, picked(n), cr), carry)
            m, l, acc = carry
            outs.append(acc / l)
        o_ref[:, cols] = jnp.where(lane < HEAD_DIM, outs[0], outs[1])


def _moba_prompt(rel_bias, qkv, kmean, t5, b, t):
    tq = MOBA_BLOCK
    nq = t // tq
    nb = kmean.shape[1]
    return pl.pallas_call(
        _moba_body, grid=(b, nq),
        in_specs=[pl.BlockSpec(memory_space=pltpu.SMEM),
                  pl.BlockSpec((tq, D_ATT), lambda i, j: (i * nq + j, 0)),
                  pl.BlockSpec((t, D_ATT), lambda i, j: (i, 1)),
                  pl.BlockSpec((t, D_ATT), lambda i, j: (i, 2)),
                  pl.BlockSpec((None, nb, D_ATT), lambda i, j: (i, 0, 0)),
                  pl.BlockSpec(t5.shape, lambda i, j: (0, 0, 0, 0))],
        out_specs=pl.BlockSpec((tq, D_ATT), lambda i, j: (i * nq + j, 0)),
        out_shape=jax.ShapeDtypeStruct((b * t, D_ATT), f32),
        compiler_params=_cp(("parallel", "arbitrary")), name="moba_prompt")(rel_bias, qkv, qkv, qkv, kmean, t5)


def _outproj_body(x_ref, a_ref, f_ref, m_ref, w_ref, o_ref):
    acc = _dot(a_ref[...].astype(bf16), w_ref[0:D_CONV, :])
    acc = acc + _dot(f_ref[...].astype(bf16), w_ref[D_CONV:D_CONV + D_ATT, :])
    acc = acc + _dot(m_ref[...].astype(bf16), w_ref[D_CONV + D_ATT:, :])
    o_ref[...] = x_ref[...] + acc


def _outproj(x, a, fo, mo, w):
    m = x.shape[0]
    tm = min(ROW_TILE, m)
    row = lambda wd: pl.BlockSpec((tm, wd), lambda i: (i, 0))
    return pl.pallas_call(
        _outproj_body, grid=(m // tm,),
        in_specs=[row(D_MODEL), row(D_CONV), row(D_ATT), row(D_ATT), pl.BlockSpec(w.shape, lambda i: (0, 0))],
        out_specs=row(D_MODEL), out_shape=jax.ShapeDtypeStruct(x.shape, f32),
        compiler_params=_cp(("parallel",)), name="outproj")(x, a, fo, mo, w)


def _router_body(x_ref, g_ref, w_ref, o_ref):
    logits = _dot(_rms(x_ref[...], g_ref[...]).astype(bf16), w_ref[...])
    lane = lax.broadcasted_iota(jnp.int32, logits.shape, 1)
    g = jnp.where(lane < N_EXPERTS, logits, -jnp.inf)
    v1 = jnp.max(g, axis=-1, keepdims=True)
    i1 = jnp.min(jnp.where(g == v1, lane, LANES), axis=-1, keepdims=True)
    g2 = jnp.where(lane == i1, -jnp.inf, g)
    v2 = jnp.max(g2, axis=-1, keepdims=True)
    i2 = jnp.min(jnp.where(g2 == v2, lane, LANES), axis=-1, keepdims=True)
    e2 = jnp.exp(v2 - v1)
    w1 = 1.0 / (1.0 + e2)
    w2 = e2 / (1.0 + e2)
    o_ref[...] = jnp.where(lane == i1, w1, 0.0) + jnp.where(lane == i2, w2, 0.0)


def _router(x, g, w_pad):
    m = x.shape[0]
    tm = min(ROW_TILE, m)
    return pl.pallas_call(
        _router_body, grid=(m // tm,),
        in_specs=[pl.BlockSpec((tm, D_MODEL), lambda i: (i, 0)), pl.BlockSpec(g.shape, lambda i: (0, 0)),
                  pl.BlockSpec(w_pad.shape, lambda i: (0, 0))],
        out_specs=pl.BlockSpec((tm, LANES), lambda i: (i, 0)),
        out_shape=jax.ShapeDtypeStruct((m, LANES), f32),
        compiler_params=_cp(("parallel",)), name="router")(x, g, w_pad)


def _ffn_body(has_comb, has_final, x_ref, g_ref, *rest):
    rest = list(rest)
    comb_ref = rest.pop(0) if has_comb else None
    wg_ref, wu_ref, wd_ref = rest[:3]
    rest = rest[3:]
    gfin_ref = rest.pop(0) if has_final else None
    o_ref, hn_s, acc_s = rest
    e, j = pl.program_id(1), pl.program_id(2)
    first = (e == 0) & (j == 0)
    last = (e == pl.num_programs(1) - 1) & (j == pl.num_programs(2) - 1)

    @pl.when(first)
    def _():
        hn_s[...] = _rms(x_ref[...], g_ref[...]).astype(bf16)
        acc_s[...] = jnp.zeros_like(acc_s)

    hn = hn_s[...]
    h = _silu(_dot(hn, wg_ref[...].astype(bf16))) * _dot(hn, wu_ref[...].astype(bf16))
    if has_comb:
        comb = comb_ref[...]
        lane = lax.broadcasted_iota(jnp.int32, comb.shape, 1)
        h = h * jnp.sum(jnp.where(lane == e, comb, 0.0), axis=-1, keepdims=True)
    acc_s[...] += _dot(h.astype(bf16), wd_ref[...].astype(bf16))

    @pl.when(last)
    def _():
        y = x_ref[...] + acc_s[...]
        o_ref[...] = _rms(y, gfin_ref[...]) if has_final else y


def _ffn(x, g, comb, wg, wu, wd, gfin):
    m = x.shape[0]
    n_e, _, n_f = wg.shape
    tm = min(FFN_ROW_TILE, m)
    tf = FFN_COL_TILE
    in_specs = [pl.BlockSpec((tm, D_MODEL), lambda i, e, j: (i, 0)), pl.BlockSpec(g.shape, lambda i, e, j: (0, 0))]
    args = [x, g]
    if comb is not None:
        in_specs.append(pl.BlockSpec((tm, LANES), lambda i, e, j: (i, 0)))
        args.append(comb)
    in_specs += [pl.BlockSpec((None, D_MODEL, tf), lambda i, e, j: (e, 0, j)),
                 pl.BlockSpec((None, D_MODEL, tf), lambda i, e, j: (e, 0, j)),
                 pl.BlockSpec((None, tf, D_MODEL), lambda i, e, j: (e, j, 0))]
    args += [wg, wu, wd]
    if gfin is not None:
        in_specs.append(pl.BlockSpec(gfin.shape, lambda i, e, j: (0, 0)))
        args.append(gfin)
    return pl.pallas_call(
        functools.partial(_ffn_body, comb is not None, gfin is not None),
        grid=(m // tm, n_e, n_f // tf),
        in_specs=in_specs,
        out_specs=pl.BlockSpec((tm, D_MODEL), lambda i, e, j: (i, 0)),
        out_shape=jax.ShapeDtypeStruct(x.shape, f32),
        scratch_shapes=[pltpu.VMEM((tm, D_MODEL), bf16), pltpu.VMEM((tm, D_MODEL), f32)],
        compiler_params=_cp(("parallel", "arbitrary", "arbitrary")), name="ffn")(*args)


def _page_spec(layer, slot, chunk=lambda c: c):
    return pl.BlockSpec((None, None, N_HEADS, HEAD_DIM, PAGE_SIZE),
                        lambda b, c, pt: (layer, pt[b, chunk(c) * DEC_PAGES + slot], 0, 0, 0))


def _logf_gather_body(pt_ref, *refs):
    o_ref = refs[-1]
    b, c = pl.program_id(0), pl.program_id(1)
    for s in range(DEC_PAGES):
        row = pt_ref[b, c * DEC_PAGES + s] % SUBLANES
        o_ref[:, pl.ds(s, 1), :] = refs[s][:, pl.ds(row, 1), :]


def _logf_gather(page_table, logf_t, layer):
    b, n_pages = page_table.shape

    def spec(slot):
        return pl.BlockSpec((None, N_HEADS, SUBLANES, PAGE_SIZE),
                            lambda i, c, pt: (layer, 0, pt[i, c * DEC_PAGES + slot] // SUBLANES, 0))

    return pl.pallas_call(
        _logf_gather_body,
        grid_spec=pltpu.PrefetchScalarGridSpec(
            num_scalar_prefetch=1, grid=(b, n_pages // DEC_PAGES),
            in_specs=[spec(s) for s in range(DEC_PAGES)],
            out_specs=pl.BlockSpec((None, N_HEADS, DEC_PAGES, PAGE_SIZE), lambda i, c, pt: (i, 0, c, 0))),
        out_shape=jax.ShapeDtypeStruct((b, N_HEADS, n_pages, PAGE_SIZE), f32),
        compiler_params=_cp(("parallel", "arbitrary")), name="logf_gather")(page_table, *([logf_t] * DEC_PAGES))


def _suffix_body(lf_ref, new_ref, o_ref):
    x = lf_ref[...]
    n = x.shape[1]
    pos = lax.broadcasted_iota(jnp.int32, x.shape, 1)
    y = x
    s = 1
    while s < n:
        y = y + jnp.where(pos + s < n, pltpu.roll(y, n - s, axis=1), 0.0)
        s *= 2
    o_ref[...] = (y - x) + new_ref[:, 0:1]


def _suffix_bias(lf_t, lf_new):
    b, _, n = lf_t.shape
    return pl.pallas_call(
        _suffix_body, grid=(b,),
        in_specs=[pl.BlockSpec((None, SUBLANES, n), lambda i: (i, 0, 0)),
                  pl.BlockSpec((None, SUBLANES, LANES), lambda i: (i, 0, 0))],
        out_specs=pl.BlockSpec((None, SUBLANES, n), lambda i: (i, 0, 0)),
        out_shape=jax.ShapeDtypeStruct(lf_t.shape, f32),
        compiler_params=_cp(("parallel",)), name="suffix_bias")(lf_t, lf_new)


def _block_diag(x):
    b = x.shape[0]
    eye = jnp.eye(SUBLANES, N_HEADS, dtype=x.dtype)
    return (eye[None, :, :, None] * x[:, None, :, :]).reshape(b, SUBLANES, D_ATT)


def _rows_of_head(x):
    eye = jnp.eye(N_HEADS, SUBLANES, dtype=x.dtype)
    return eye[None, :, :, None] * x[:, :, None, :]


def _flat_page(ref):
    return ref[...].reshape(D_ATT, PAGE_SIZE).astype(bf16)


def _fox_dec_body(pt_ref, q_ref, kn_ref, vn_ref, bias_ref, *refs):
    k_refs, v_refs = refs[:DEC_PAGES], refs[DEC_PAGES:2 * DEC_PAGES]
    o_ref, s_all, m_s, inv_s, acc_s = refs[2 * DEC_PAGES:]
    nc = s_all.shape[0]
    c = pl.program_id(1)
    q = q_ref[...]

    @pl.when(c == 0)
    def _():
        m_s[...] = jnp.full_like(m_s, -jnp.inf)

    @pl.when(c < nc)
    def _():
        qb = q.astype(bf16)
        s = jnp.concatenate([_dot(qb, _flat_page(k_refs[g])) for g in range(DEC_PAGES)], axis=1) + bias_ref[...]
        s_all[c] = s
        m_s[...] = jnp.maximum(m_s[...], jnp.max(s, axis=-1, keepdims=True))

    @pl.when(c == nc)
    def _():
        s_new = jnp.sum(_round_bf16(q) * _round_bf16(kn_ref[...]), axis=-1, keepdims=True)
        m = jnp.maximum(m_s[...], s_new)
        p_new = jnp.exp(s_new - m)
        l = p_new
        for n in range(nc):
            p = jnp.exp(s_all[n] - m)
            s_all[n] = p
            l = l + jnp.sum(p, axis=-1, keepdims=True)
        inv = 1.0 / l
        inv_s[...] = inv
        acc_s[...] = _round_bf16(p_new * inv) * _round_bf16(vn_ref[...])

    @pl.when(c >= nc)
    def _():
        p = (s_all[c - nc] * inv_s[...]).astype(bf16)
        pv = acc_s[...]
        for g in range(DEC_PAGES):
            pv = pv + _dot_nt(p[:, g * PAGE_SIZE:(g + 1) * PAGE_SIZE], _flat_page(v_refs[g]))
        acc_s[...] = pv

    @pl.when(c == 2 * nc - 1)
    def _():
        row = lax.broadcasted_iota(jnp.int32, acc_s.shape, 0)
        col = lax.broadcasted_iota(jnp.int32, acc_s.shape, 1)
        own = (col >= row * HEAD_DIM) & (col < (row + 1) * HEAD_DIM)
        o_ref[...] = jnp.sum(jnp.where(own, acc_s[...], 0.0), axis=0, keepdims=True)


def _fox_decode(page_table, q, k_new, v_new, bias, cache_k, cache_v, layer):
    b, n_pages = page_table.shape
    nc = n_pages // DEC_PAGES
    small = pl.BlockSpec((None, SUBLANES, D_ATT), lambda i, c, pt: (i, 0, 0))
    k_chunk = lambda c: jnp.minimum(c, nc - 1)
    v_chunk = lambda c: jnp.maximum(c - nc, 0)
    return pl.pallas_call(
        _fox_dec_body,
        grid_spec=pltpu.PrefetchScalarGridSpec(
            num_scalar_prefetch=1, grid=(b, 2 * nc),
            in_specs=[small, small, small,
                      pl.BlockSpec((None, SUBLANES, DEC_PAGES * PAGE_SIZE), lambda i, c, pt: (i, 0, k_chunk(c)))]
            + [_page_spec(layer, s, k_chunk) for s in range(DEC_PAGES)]
            + [_page_spec(layer, s, v_chunk) for s in range(DEC_PAGES)],
            out_specs=pl.BlockSpec((None, 1, D_ATT), lambda i, c, pt: (i, 0, 0)),
            scratch_shapes=[pltpu.VMEM((nc, SUBLANES, DEC_PAGES * PAGE_SIZE), f32), pltpu.VMEM((SUBLANES, 1), f32),
                            pltpu.VMEM((SUBLANES, 1), f32), pltpu.VMEM((SUBLANES, D_ATT), f32)]),
        out_shape=jax.ShapeDtypeStruct((b, 1, D_ATT), f32),
        compiler_params=_cp(("parallel", "arbitrary")), name="fox_decode")(
            page_table, q, k_new, v_new, bias, *([cache_k] * DEC_PAGES), *([cache_v] * DEC_PAGES))


def _gate_body(pt_ref, q_ref, *refs):
    o_ref = refs[-1]
    pages_per_block = MOBA_BLOCK // PAGE_SIZE
    q = _round_bf16(q_ref[...])
    for blk in range(DEC_PAGES // pages_per_block):
        ksum = refs[blk * pages_per_block][...]
        for s in range(1, pages_per_block):
            ksum = ksum + refs[blk * pages_per_block + s][...]
        kmean = jnp.sum(ksum, axis=-1, keepdims=True) * (1.0 / MOBA_BLOCK)
        o_ref[blk] = jnp.sum(_round_bf16(kmean) * q, axis=1)


def _gate_decode(page_table, q_col, cache_k, layer):
    b, n_pages = page_table.shape
    per_step = DEC_PAGES * PAGE_SIZE // MOBA_BLOCK
    return pl.pallas_call(
        _gate_body,
        grid_spec=pltpu.PrefetchScalarGridSpec(
            num_scalar_prefetch=1, grid=(b, n_pages // DEC_PAGES),
            in_specs=[pl.BlockSpec((None, N_HEADS, HEAD_DIM, 1), lambda i, c, pt: (i, 0, 0, 0))]
            + [_page_spec(layer, s) for s in range(DEC_PAGES)],
            out_specs=pl.BlockSpec((None, per_step, N_HEADS, 1), lambda i, c, pt: (i, c, 0, 0))),
        out_shape=jax.ShapeDtypeStruct((b, n_pages * PAGE_SIZE // MOBA_BLOCK, N_HEADS, 1), f32),
        compiler_params=_cp(("parallel", "arbitrary")), name="gate_decode")(page_table, q_col, *([cache_k] * DEC_PAGES))


def _select_body(g_ref, o_ref):
    g = g_ref[...]
    nb = g.shape[0]
    blk = lax.broadcasted_iota(jnp.int32, g.shape, 0)
    for t in range(MOBA_TOPK):
        mx = jnp.max(g, axis=0, keepdims=True)
        idx = jnp.min(jnp.where(g == mx, blk, nb), axis=0, keepdims=True)
        o_ref[t] = idx[0]
        g = jnp.where(blk == idx, -jnp.inf, g)


def _select_decode(gates):
    b, nb = gates.shape[:2]
    return pl.pallas_call(
        _select_body, grid=(b,),
        in_specs=[pl.BlockSpec((None, nb, N_HEADS, 1), lambda i: (i, 0, 0, 0))],
        out_specs=pl.BlockSpec((None, MOBA_TOPK, N_HEADS, 1), lambda i: (i, 0, 0, 0)),
        out_shape=jax.ShapeDtypeStruct((b, MOBA_TOPK, N_HEADS, 1), jnp.int32),
        compiler_params=_cp(("parallel",)), name="select_decode")(gates)


def _moba_dec_body(past_len, pt_ref, idx_ref, q_ref, kn_ref, vn_ref, tab_ref, *refs):
    pages_per_block = MOBA_BLOCK // PAGE_SIZE
    n_sel = MOBA_TOPK * pages_per_block
    k_refs, v_refs, o_ref = refs[:n_sel], refs[n_sel:2 * n_sel], refs[2 * n_sel]
    b, h = pl.program_id(0), pl.program_id(1)
    q = q_ref[...]
    qb = q.astype(bf16)
    tab = tab_ref[...]
    lane = lax.broadcasted_iota(jnp.int32, (1, PAGE_SIZE), 1)

    def t5(dist):
        bucket = _t5_bucket(dist)
        out = jnp.zeros((SUBLANES, dist.shape[1]), f32)
        for nbk in range(N_BUCKETS):
            out = jnp.where(bucket == nbk, tab[:, nbk:nbk + 1], out)
        return out

    scores = []
    for t in range(MOBA_TOPK):
        blk = idx_ref[b, t, h]
        for s in range(pages_per_block):
            kpos = blk * MOBA_BLOCK + s * PAGE_SIZE + lane
            scores.append(_dot(qb, k_refs[t * pages_per_block + s][...].astype(bf16)) + t5(past_len - kpos))
    s_past = jnp.concatenate(scores, axis=1)
    s_new = (jnp.sum(_round_bf16(q) * _round_bf16(kn_ref[...]), axis=-1, keepdims=True)
             + t5(jnp.zeros((1, 1), jnp.int32)))
    m = jnp.maximum(jnp.max(s_past, axis=-1, keepdims=True), s_new)
    p = jnp.exp(s_past - m)
    p_new = jnp.exp(s_new - m)
    inv = 1.0 / (jnp.sum(p, axis=-1, keepdims=True) + p_new)
    p = (p * inv).astype(bf16)
    acc = _round_bf16(p_new * inv) * _round_bf16(vn_ref[...])
    for g in range(n_sel):
        acc = acc + _dot_nt(p[:, g * PAGE_SIZE:(g + 1) * PAGE_SIZE], v_refs[g][...].astype(bf16))
    row = lax.broadcasted_iota(jnp.int32, (SUBLANES, HEAD_DIM), 0)
    res = jnp.where(row == h, acc, 0.0)

    @pl.when(h == 0)
    def _():
        o_ref[...] = res

    @pl.when(h > 0)
    def _():
        o_ref[...] += res


def _moba_decode(page_table, idx, q, k_new, v_new, tab_t, cache_k, cache_v, layer):
    b, n_pages = page_table.shape
    pages_per_block = MOBA_BLOCK // PAGE_SIZE
    small = pl.BlockSpec((None, None, SUBLANES, HEAD_DIM), lambda i, h, pt, ix: (i, h, 0, 0))

    def sel_spec(t, s):
        return pl.BlockSpec((None, None, None, HEAD_DIM, PAGE_SIZE),
                            lambda i, h, pt, ix: (layer, pt[i, ix[i, t, h] * pages_per_block + s], h, 0, 0))

    specs = [sel_spec(t, s) for t in range(MOBA_TOPK) for s in range(pages_per_block)]
    return pl.pallas_call(
        functools.partial(_moba_dec_body, n_pages * PAGE_SIZE),
        grid_spec=pltpu.PrefetchScalarGridSpec(
            num_scalar_prefetch=2, grid=(b, N_HEADS),
            in_specs=[small, small, small, pl.BlockSpec(tab_t.shape, lambda i, h, pt, ix: (0, 0))] + specs * 2,
            out_specs=pl.BlockSpec((None, SUBLANES, HEAD_DIM), lambda i, h, pt, ix: (i, 0, 0))),
        out_shape=jax.ShapeDtypeStruct((b, SUBLANES, HEAD_DIM), f32),
        compiler_params=_cp(("parallel", "arbitrary")), name="moba_decode")(
            page_table, idx, q, k_new, v_new, tab_t, *([cache_k] * len(specs)), *([cache_v] * len(specs)))


def _split_w_in(w):
    o_f = 2 * D_CONV
    o_g = o_f + 3 * D_ATT
    o_m = o_g + N_HEADS
    wg = jnp.zeros((D_MODEL, LANES), f32).at[:, :N_HEADS].set(w[:, o_g:o_m])
    return w[:, :o_f].astype(bf16), w[:, o_f:o_g].astype(bf16), w[:, o_m:].astype(bf16), wg.astype(bf16)


def _pad_lanes(v):
    return jnp.zeros((1, LANES), f32).at[0, :v.shape[0]].set(v)


def _layer_weights(l, p):
    wglu, wf, wm, wg = _split_w_in(p["w_in"][l])
    i = l // 2
    d = dict(norm_mix=p["norm_mix"][l][None], norm_ffn=p["norm_ffn"][l][None],
             wglu=wglu, wf=wf, wm=wm, wg=wg, bfg=_pad_lanes(p["b_forget"][l]),
             conv_w=p["conv_w"][l], conv_b=p["conv_b"][l][None], ln_g=p["conv_ln_g"][l][None], ln_b=p["conv_ln_b"][l][None],
             w_pw=p["w_conv_pw"][l].astype(bf16), w_out=p["w_out"][l].astype(bf16))
    if l % 2 == 0:
        d.update(ffn_g=p["w_dense_gate"][i][None], ffn_u=p["w_dense_up"][i][None], ffn_d=p["w_dense_down"][i][None], router=None)
    else:
        d.update(ffn_g=p["w_moe_gate"][i], ffn_u=p["w_moe_up"][i], ffn_d=p["w_moe_down"][i],
                 router=jnp.zeros((D_MODEL, LANES), f32).at[:, :N_EXPERTS].set(p["w_router"][i]).astype(bf16))
    return d


def _channel_mix(x, w, gfin):
    comb = None if w["router"] is None else _router(x, w["norm_ffn"], w["router"])
    return _ffn(x, w["norm_ffn"], comb, w["ffn_g"], w["ffn_u"], w["ffn_d"], gfin)


def _heads(a, b, t):
    return a.reshape(b, t, N_HEADS, HEAD_DIM)


def _prompt_layer(x, w, rel_bias, t5, b, t, gfin):
    u, ff, fm, bff, bfm, lf, lft = _proj(x, w["norm_mix"], w["wglu"], w["wf"], w["wm"], w["wg"], w["bfg"], True)
    u3 = u.reshape(b, t, D_CONV)
    up = jnp.concatenate([jnp.zeros((b, CONV_HALO, D_CONV), f32), u3], axis=1)
    conv_out = _conv_prompt(up, w["conv_w"], w["conv_b"], w["ln_g"], w["ln_b"], w["w_pw"], t).reshape(b * t, D_CONV)
    ccol, crow = _cumsum(lf, lft, b, t)
    nq = t // min(ATT_TILE, t)
    crow3 = crow.reshape(SUBLANES, b * nq, t // nq).transpose(1, 0, 2)
    fox_out = _fox_prompt(bff, ccol, crow3, b, t)
    kmean = _kmean_prompt(fm, b, t)
    moba_out = _moba_prompt(rel_bias, bfm, kmean, t5, b, t)
    x = _outproj(x, conv_out, fox_out, moba_out, w["w_out"])
    x = _channel_mix(x, w, gfin)
    state = (_heads(ff[:, D_ATT:2 * D_ATT], b, t), _heads(ff[:, 2 * D_ATT:], b, t), lf[:, :N_HEADS].reshape(b, t, N_HEADS),
             _heads(fm[:, D_ATT:2 * D_ATT], b, t), _heads(fm[:, 2 * D_ATT:], b, t), u3[:, t - (CONV_WIDTH - 1):])
    return x, state


def _sample_layer(x, w, l, rel_bias, caches, page_table, gfin):
    cache_fk, cache_fv, logf_t, cache_mk, cache_mv, state_t = caches
    b = x.shape[0]
    n_past = page_table.shape[1] * PAGE_SIZE
    u, ff, fm, _, _, lf = _proj(x, w["norm_mix"], w["wglu"], w["wf"], w["wm"], w["wg"], w["bfg"], False)
    conv_out = _conv_step(state_t[l], u, w["conv_w"], w["conv_b"], w["ln_g"], w["ln_b"], w["w_pw"])
    heads = lambda a: a.reshape(b, N_HEADS, HEAD_DIM)
    fq, fk, fv = heads(ff[:, :D_ATT]) * Q_SCALE, heads(ff[:, D_ATT:2 * D_ATT]), heads(ff[:, 2 * D_ATT:])
    mq, mk, mv = heads(fm[:, :D_ATT]) * Q_SCALE, heads(fm[:, D_ATT:2 * D_ATT]), heads(fm[:, 2 * D_ATT:])
    lf_pages = _logf_gather(page_table, logf_t, l)
    lf_t = jnp.pad(lf_pages.reshape(b, N_HEADS, n_past), ((0, 0), (0, SUBLANES - N_HEADS), (0, 0)))
    lf_new = jnp.broadcast_to(jnp.pad(lf[:, :N_HEADS], ((0, 0), (0, SUBLANES - N_HEADS)))[:, :, None], (b, SUBLANES, LANES))
    bias = _suffix_bias(lf_t, lf_new)
    fox_out = _fox_decode(page_table, _block_diag(fq), _block_diag(fk), _block_diag(fv), bias,
                          cache_fk, cache_fv, l).reshape(b, D_ATT)
    idx = _select_decode(_gate_decode(page_table, mq[..., None], cache_mk, l))[..., 0]
    tab_t = jnp.pad(rel_bias.T, ((0, SUBLANES - N_HEADS), (0, 0)))
    moba_out = _moba_decode(page_table, idx, _rows_of_head(mq), _rows_of_head(mk), _rows_of_head(mv), tab_t,
                            cache_mk, cache_mv, l)[:, :N_HEADS].reshape(b, D_ATT)
    x = _outproj(x, conv_out, fox_out, moba_out, w["w_out"])
    x = _channel_mix(x, w, gfin)
    conv_state_t = jnp.concatenate([state_t[l][1:], u[None]], axis=0)
    state = (fk[:, None], fv[:, None], lf[:, None, :N_HEADS], mk[:, None], mv[:, None], conv_state_t)
    return x, state


def kernel(x_prompt, x_sample, cache_fox_k, cache_fox_v, cache_fox_logf, cache_moba_k, cache_moba_v, state_conv, page_table, norm_mix, norm_ffn, norm_final, w_in, b_forget, conv_w, conv_b, conv_ln_g, conv_ln_b, w_conv_pw, rel_bias, w_out, w_dense_gate, w_dense_up, w_dense_down, w_router, w_moe_gate, w_moe_up, w_moe_down):
    p = dict(norm_mix=norm_mix, norm_ffn=norm_ffn, w_in=w_in, b_forget=b_forget, conv_w=conv_w, conv_b=conv_b,
             conv_ln_g=conv_ln_g, conv_ln_b=conv_ln_b, w_conv_pw=w_conv_pw, w_out=w_out, w_dense_gate=w_dense_gate,
             w_dense_up=w_dense_up, w_dense_down=w_dense_down, w_router=w_router, w_moe_gate=w_moe_gate,
             w_moe_up=w_moe_up, w_moe_down=w_moe_down)
    depth = norm_mix.shape[0]
    weights = [_layer_weights(l, p) for l in range(depth)]
    gfin = norm_final[None]
    t5 = _t5_tiles(rel_bias)

    b, t, _ = x_prompt.shape
    x = x_prompt.reshape(b * t, D_MODEL)
    states_p = []
    for l in range(depth):
        x, st = _prompt_layer(x, weights[l], rel_bias, t5, b, t, gfin if l == depth - 1 else None)
        states_p.append(st)
    y_prompt = x.reshape(b, t, D_MODEL)

    bs = x_sample.shape[0]
    rows_on_lanes = lambda c: jnp.transpose(c, (0, 1, 3, 4, 2))
    caches = (rows_on_lanes(cache_fox_k), rows_on_lanes(cache_fox_v), jnp.transpose(cache_fox_logf, (0, 3, 1, 2)),
              rows_on_lanes(cache_moba_k), rows_on_lanes(cache_moba_v), jnp.transpose(state_conv, (0, 2, 1, 3)))
    x = x_sample.reshape(bs, D_MODEL)
    states_s = []
    for l in range(depth):
        x, st = _sample_layer(x, weights[l], l, rel_bias, caches, page_table, gfin if l == depth - 1 else None)
        states_s.append(st)
    y_sample = x.reshape(bs, 1, D_MODEL)

    stack = lambda sts: [jnp.stack(s) for s in zip(*sts)]
    out_s = stack(states_s)
    out_s[-1] = jnp.transpose(out_s[-1], (0, 2, 1, 3))
    return (y_prompt, y_sample, *stack(states_p), *out_s)
```

```python
import functools
import math

import jax
import jax.numpy as jnp
from jax import lax
from jax.experimental import pallas as pl
from jax.experimental.pallas import tpu as pltpu

f32 = jnp.float32
bf16 = jnp.bfloat16

D_MODEL = 1024
HEAD_DIM = 64
N_HEADS = 6
D_ATT = N_HEADS * HEAD_DIM
D_CONV = 256
CONV_WIDTH = 31
PAGE_SIZE = 128
MOBA_BLOCK = 256
MOBA_TOPK = 3
N_BUCKETS = 32
MAX_DISTANCE = 128
N_EXPERTS = 8
Q_SCALE = HEAD_DIM ** -0.5

LANES = 128
SUBLANES = 8
VMEM_LIMIT = 56 * 1024 * 1024

ROW_TILE = 256
ATT_TILE = 256
ATT_HEAD_GROUP = 6
CONV_TILE = 512
CONV_HALO = 32
FFN_ROW_TILE = 1024
FFN_COL_TILE = 256
DEC_PAGES = 16


def _cp(sem):
    return pltpu.CompilerParams(dimension_semantics=sem, vmem_limit_bytes=VMEM_LIMIT)


def _rms(x, g):
    return (x * lax.rsqrt(jnp.mean(x * x, axis=-1, keepdims=True) + 1e-6)) * g


def _log_sigmoid(x):
    return jnp.minimum(x, 0.0) - jnp.log(1.0 + jnp.exp(-jnp.abs(x)))


def _silu(x):
    return x * jax.nn.sigmoid(x)


def _round_bf16(x):
    return x.astype(bf16).astype(f32)


def _dot(a, b):
    return jnp.dot(a, b, preferred_element_type=f32)


def _dot_nt(a, b, precision=None):
    return lax.dot_general(a, b, (((1,), (1,)), ((), ())), preferred_element_type=f32, precision=precision)


def _t5_bucket(d):
    max_exact = N_BUCKETS // 2
    df = jnp.maximum(d, max_exact).astype(f32)
    large = max_exact + (jnp.log(df / max_exact) / math.log(MAX_DISTANCE / max_exact)
                         * (N_BUCKETS - max_exact)).astype(jnp.int32)
    return jnp.where(d < max_exact, d, jnp.minimum(large, N_BUCKETS - 1))


def _proj_body(with_t, x_ref, g_ref, wglu_ref, wf_ref, wm_ref, wg_ref, bfg_ref,
               u_ref, ff_ref, fm_ref, bff_ref, bfm_ref, lf_ref, *maybe_lft):
    hb = _rms(x_ref[...], g_ref[...]).astype(bf16)
    glu = _dot(hb, wglu_ref[...])
    u_ref[...] = glu[:, :D_CONV] * jax.nn.sigmoid(glu[:, D_CONV:])
    col = lax.broadcasted_iota(jnp.int32, (1, 3 * D_ATT), 1)
    qscale = jnp.where(col < D_ATT, Q_SCALE, 1.0).astype(f32)
    pf = _dot(hb, wf_ref[...])
    ff_ref[...] = pf
    bff_ref[...] = (pf * qscale).astype(bf16)
    pm = _dot(hb, wm_ref[...])
    fm_ref[...] = pm
    bfm_ref[...] = (pm * qscale).astype(bf16)
    lf = _log_sigmoid(_dot(hb, wg_ref[...]) + bfg_ref[...])
    lf_ref[...] = lf
    if with_t:
        lft_ref, fvt_ref, mvt_ref = maybe_lft
        lft_ref[...] = lf.T[:SUBLANES]
        fvt_ref[...] = pf[:, 2 * D_ATT:].T.astype(bf16)
        mvt_ref[...] = pm[:, 2 * D_ATT:].T.astype(bf16)


def _proj(x, g, wglu, wf, wm, wg, bfg, with_t):
    m = x.shape[0]
    tm = min(ROW_TILE, m)
    row = lambda w: pl.BlockSpec((tm, w), lambda i: (i, 0))
    full = lambda a: pl.BlockSpec(a.shape, lambda i: (0, 0))
    out_shape = [jax.ShapeDtypeStruct((m, D_CONV), f32),
                 jax.ShapeDtypeStruct((m, 3 * D_ATT), f32), jax.ShapeDtypeStruct((m, 3 * D_ATT), f32),
                 jax.ShapeDtypeStruct((m, 3 * D_ATT), bf16), jax.ShapeDtypeStruct((m, 3 * D_ATT), bf16),
                 jax.ShapeDtypeStruct((m, LANES), f32)]
    out_specs = [row(D_CONV), row(3 * D_ATT), row(3 * D_ATT), row(3 * D_ATT), row(3 * D_ATT), row(LANES)]
    if with_t:
        out_shape.append(jax.ShapeDtypeStruct((SUBLANES, m), f32))
        out_specs.append(pl.BlockSpec((SUBLANES, tm), lambda i: (0, i)))
        for _ in range(2):
            out_shape.append(jax.ShapeDtypeStruct((m // tm, D_ATT, tm), bf16))
            out_specs.append(pl.BlockSpec((None, D_ATT, tm), lambda i: (i, 0, 0)))
    return pl.pallas_call(
        functools.partial(_proj_body, with_t),
        grid=(m // tm,),
        in_specs=[row(D_MODEL), full(g), full(wglu), full(wf), full(wm), full(wg), full(bfg)],
        out_specs=out_specs, out_shape=out_shape,
        compiler_params=_cp(("parallel",)), name="proj")(x, g, wglu, wf, wm, wg, bfg)


def _cumsum_body(lf_ref, lft_ref, ccol_ref, crow_ref):
    x = lf_ref[...]
    t = x.shape[0]
    pos = lax.broadcasted_iota(jnp.int32, x.shape, 0)
    s = 1
    while s < t:
        x = x + jnp.where(pos >= s, pltpu.roll(x, s, axis=0), 0.0)
        s *= 2
    ccol_ref[...] = x
    y = lft_ref[...]
    pos = lax.broadcasted_iota(jnp.int32, y.shape, 1)
    s = 1
    while s < t:
        y = y + jnp.where(pos >= s, pltpu.roll(y, s, axis=1), 0.0)
        s *= 2
    crow_ref[...] = y


def _cumsum(lf, lft, b, t):
    return pl.pallas_call(
        _cumsum_body, grid=(b,),
        in_specs=[pl.BlockSpec((t, LANES), lambda i: (i, 0)), pl.BlockSpec((SUBLANES, t), lambda i: (0, i))],
        out_specs=[pl.BlockSpec((t, LANES), lambda i: (i, 0)), pl.BlockSpec((SUBLANES, t), lambda i: (0, i))],
        out_shape=[jax.ShapeDtypeStruct(lf.shape, f32), jax.ShapeDtypeStruct(lft.shape, f32)],
        compiler_params=_cp(("parallel",)), name="cumsum")(lf, lft)


def _conv_tail(y, lng_ref, lnb_ref, wpw_ref):
    mu = jnp.mean(y, axis=-1, keepdims=True)
    var = jnp.mean(jnp.square(y - mu), axis=-1, keepdims=True)
    z = _silu((y - mu) * lax.rsqrt(var + 1e-5) * lng_ref[...] + lnb_ref[...])
    return _dot(z.astype(bf16), wpw_ref[...])


def _conv_body(main_ref, halo_ref, w_ref, b_ref, lng_ref, lnb_ref, wpw_ref, o_ref):
    tt = main_ref.shape[0]
    win = _round_bf16(jnp.concatenate([main_ref[...], halo_ref[...]], axis=0))
    w = _round_bf16(w_ref[...])
    lead = CONV_HALO - (CONV_WIDTH - 1)
    acc = jnp.zeros((tt, D_CONV), f32) + b_ref[...]
    for k in range(CONV_WIDTH):
        acc = acc + win[lead + k:lead + k + tt] * w[k:k + 1]
    o_ref[...] = _conv_tail(acc, lng_ref, lnb_ref, wpw_ref)


def _conv_prompt(up, w, b, lng, lnb, wpw, t):
    bsz = up.shape[0]
    tt = min(CONV_TILE, t)
    full = lambda a: pl.BlockSpec(a.shape, lambda i, j: (0, 0))
    return pl.pallas_call(
        _conv_body, grid=(bsz, t // tt),
        in_specs=[pl.BlockSpec((None, tt, D_CONV), lambda i, j: (i, j, 0)),
                  pl.BlockSpec((None, CONV_HALO, D_CONV), lambda i, j: (i, (j + 1) * (tt // CONV_HALO), 0)),
                  full(w), full(b), full(lng), full(lnb), full(wpw)],
        out_specs=pl.BlockSpec((None, tt, D_CONV), lambda i, j: (i, j, 0)),
        out_shape=jax.ShapeDtypeStruct((bsz, t, D_CONV), f32),
        compiler_params=_cp(("parallel", "parallel")), name="conv_prompt")(up, up, w, b, lng, lnb, wpw)


def _conv_step_body(st_ref, u_ref, w_ref, b_ref, lng_ref, lnb_ref, wpw_ref, o_ref):
    w = w_ref[...]
    acc = u_ref[...] * w[CONV_WIDTH - 1:CONV_WIDTH] + b_ref[...]
    for k in range(CONV_WIDTH - 1):
        acc = acc + st_ref[k] * w[k:k + 1]
    o_ref[...] = _conv_tail(acc, lng_ref, lnb_ref, wpw_ref)


def _conv_step(state_t, u, w, b, lng, lnb, wpw):
    return pl.pallas_call(
        _conv_step_body, out_shape=jax.ShapeDtypeStruct(u.shape, f32), name="conv_step")(state_t, u, w, b, lng, lnb, wpw)


def _flash_step(qh, k, vt, bias_row, bias_col, bias_tile, mask, carry):
    m, l, acc = carry
    s = _dot_nt(k, qh)
    if bias_tile is not None:
        s = s + bias_tile
    if bias_col is not None:
        s = s - bias_col
    if bias_row is not None:
        s = s + bias_row
    if mask is not None:
        s = jnp.where(mask, s, -jnp.inf)
    m_new = jnp.maximum(m, jnp.max(s, axis=0, keepdims=True))
    alpha = jnp.exp(m - m_new)
    p = jnp.exp(s - m_new)
    l = alpha * l + jnp.sum(p, axis=0, keepdims=True)
    acc = alpha * acc + _dot(vt, p.astype(bf16))
    return m_new, l, acc


def _flash_init(tq):
    return (jnp.full((1, tq), -jnp.inf, f32), jnp.zeros((1, tq), f32), jnp.zeros((HEAD_DIM, tq), f32))


def _masked_queries(q_ref, h):
    j, hh = divmod(h, 2)
    lane = lax.broadcasted_iota(jnp.int32, (1, LANES), 1)
    qp = q_ref[:, j * LANES:(j + 1) * LANES]
    return jnp.where((lane >= hh * HEAD_DIM) & (lane < (hh + 1) * HEAD_DIM), qp, jnp.zeros_like(qp))


def _pair_cols(h):
    return slice((h // 2) * LANES, (h // 2 + 1) * LANES)


def _head_rows(h):
    return slice(h * HEAD_DIM, (h + 1) * HEAD_DIM)


def _fox_body(q_ref, k_ref, vt_ref, ccol_ref, crow_ref, o_ref):
    tq = q_ref.shape[0]
    i = pl.program_id(1)
    key = lax.broadcasted_iota(jnp.int32, (tq, tq), 0)
    qry = lax.broadcasted_iota(jnp.int32, (tq, tq), 1)
    causal = key <= qry
    outs = []
    for g0 in range(0, N_HEADS, ATT_HEAD_GROUP):
        group = range(g0, g0 + ATT_HEAD_GROUP)
        qhs = [_masked_queries(q_ref, h) for h in group]
        cqs = [crow_ref[i, h:h + 1, :] for h in group]

        def tile(n, carries, mask, group=group, qhs=qhs, cqs=cqs):
            rows = pl.ds(pl.multiple_of(n * tq, tq), tq)
            new = []
            for h, qh, cq, carry in zip(group, qhs, cqs, carries):
                ck = ccol_ref[rows, h:h + 1]
                new.append(_flash_step(qh, k_ref[rows, _pair_cols(h)], vt_ref[n, _head_rows(h), :], cq, ck, None, mask, carry))
            return tuple(new)

        init = tuple(_flash_init(tq) for _ in group)
        carries = lax.fori_loop(0, i, lambda n, cr, tile=tile: tile(n, cr, None), init)
        outs += [acc / l for _, l, acc in tile(i, carries, causal)]
    o_ref[...] = jnp.concatenate(outs, axis=0).T


def _fox_prompt(qkv, vt, ccol, crow3, b, t):
    tq = min(ATT_TILE, t)
    nq = t // tq
    return pl.pallas_call(
        _fox_body, grid=(b, nq),
        in_specs=[pl.BlockSpec((tq, D_ATT), lambda i, j: (i * nq + j, 0)),
                  pl.BlockSpec((t, D_ATT), lambda i, j: (i, 1)),
                  pl.BlockSpec((nq, D_ATT, tq), lambda i, j: (i, 0, 0)),
                  pl.BlockSpec((t, LANES), lambda i, j: (i, 0)),
                  pl.BlockSpec((nq, SUBLANES, tq), lambda i, j: (i, 0, 0))],
        out_specs=pl.BlockSpec((tq, D_ATT), lambda i, j: (i * nq + j, 0)),
        out_shape=jax.ShapeDtypeStruct((b * t, D_ATT), f32),
        compiler_params=_cp(("parallel", "arbitrary")), name="fox_prompt")(qkv, qkv, vt, ccol, crow3)


def _kmean_body(k_ref, o_ref):
    t = k_ref.shape[0]
    k = k_ref[...].reshape(t // MOBA_BLOCK, MOBA_BLOCK, D_ATT)
    o_ref[...] = jnp.mean(k, axis=1)


def _kmean_prompt(mqkv_f32, b, t):
    nb = t // MOBA_BLOCK
    return pl.pallas_call(
        _kmean_body, grid=(b,),
        in_specs=[pl.BlockSpec((t, D_ATT), lambda i: (i, 1))],
        out_specs=pl.BlockSpec((None, nb, D_ATT), lambda i: (i, 0, 0)),
        out_shape=jax.ShapeDtypeStruct((b, nb, D_ATT), f32),
        compiler_params=_cp(("parallel",)), name="kmean_prompt")(mqkv_f32)


def _t5_tiles_body(tab_ref, o_ref):
    h = pl.program_id(0)
    key = lax.broadcasted_iota(jnp.int32, (MOBA_BLOCK, MOBA_BLOCK), 0)
    qry = lax.broadcasted_iota(jnp.int32, (MOBA_BLOCK, MOBA_BLOCK), 1)
    for which in range(2):
        bucket = _t5_bucket(jnp.maximum(qry - key + which * MOBA_BLOCK, 0))
        acc = jnp.zeros((MOBA_BLOCK, MOBA_BLOCK), f32)
        for nbk in range(N_BUCKETS):
            acc = jnp.where(bucket == nbk, tab_ref[nbk, h], acc)
        o_ref[which] = acc


def _t5_tiles(rel_bias):
    return pl.pallas_call(
        _t5_tiles_body, grid=(N_HEADS,),
        in_specs=[pl.BlockSpec(memory_space=pltpu.SMEM)],
        out_specs=pl.BlockSpec((None, 2, MOBA_BLOCK, MOBA_BLOCK), lambda h: (h, 0, 0, 0)),
        out_shape=jax.ShapeDtypeStruct((N_HEADS, 2, MOBA_BLOCK, MOBA_BLOCK), f32),
        compiler_params=_cp(("parallel",)), name="t5_tiles")(rel_bias)


def _top_blocks(gate, n_valid):
    blk = lax.broadcasted_iota(jnp.int32, gate.shape, 0)
    nb = gate.shape[0]
    g = jnp.where(blk < n_valid, gate, -jnp.inf)
    sel = jnp.zeros(gate.shape, f32)
    for _ in range(min(MOBA_TOPK, nb)):
        mx = jnp.max(g, axis=0, keepdims=True)
        idx = jnp.min(jnp.where(g == mx, blk, nb), axis=0, keepdims=True)
        pick = (blk == idx) & (mx > -jnp.inf)
        sel = jnp.where(pick, 1.0, sel)
        g = jnp.where(blk == idx, -jnp.inf, g)
    return sel


def _moba_body(tab_ref, q_ref, k_ref, vt_ref, kmean_ref, t5_ref, o_ref):
    tq = q_ref.shape[0]
    nb = kmean_ref.shape[0]
    i = pl.program_id(1)
    key = lax.broadcasted_iota(jnp.int32, (tq, tq), 0)
    qry = lax.broadcasted_iota(jnp.int32, (tq, tq), 1)
    causal = key <= qry
    blk = lax.broadcasted_iota(jnp.int32, (nb, tq), 0)
    prev = jnp.maximum(i - 1, 0)
    outs = []
    for g0 in range(0, N_HEADS, ATT_HEAD_GROUP):
        group = range(g0, g0 + ATT_HEAD_GROUP)
        qhs = [_masked_queries(q_ref, h) for h in group]
        sels = [_top_blocks(_dot_nt(kmean_ref[:, _pair_cols(h)].astype(bf16), qh), i) for h, qh in zip(group, qhs)]

        def picked(n, sel):
            return jnp.max(jnp.where(blk == n, sel, 0.0), axis=0, keepdims=True) > 0.5

        def tile(n, which, own, carries, group=group, qhs=qhs, sels=sels):
            rows = pl.ds(pl.multiple_of(n * tq, tq), tq)
            new = []
            for h, qh, sel, carry in zip(group, qhs, sels, carries):
                bias_tile = None if which is None else t5_ref[h, which]
                bias_row = tab_ref[N_BUCKETS - 1, h] if which is None else None
                mask = causal if own else picked(n, sel)
                new.append(_flash_step(qh, k_ref[rows, _pair_cols(h)], vt_ref[n, _head_rows(h), :],
                                       bias_row, None, bias_tile, mask, carry))
            return tuple(new)

        carries = tile(i, 0, True, tuple(_flash_init(tq) for _ in group))
        carries = tile(prev, 1, False, carries)
        carries = lax.fori_loop(0, prev, lambda n, cr, tile=tile: tile(n, None, False, cr), carries)
        outs += [acc / l for _, l, acc in carries]
    o_ref[...] = jnp.concatenate(outs, axis=0).T


def _moba_prompt(rel_bias, qkv, vt, kmean, t5, b, t):
    tq = MOBA_BLOCK
    nq = t // tq
    nb = kmean.shape[1]
    return pl.pallas_call(
        _moba_body, grid=(b, nq),
        in_specs=[pl.BlockSpec(memory_space=pltpu.SMEM),
                  pl.BlockSpec((tq, D_ATT), lambda i, j: (i * nq + j, 0)),
                  pl.BlockSpec((t, D_ATT), lambda i, j: (i, 1)),
                  pl.BlockSpec((nq, D_ATT, tq), lambda i, j: (i, 0, 0)),
                  pl.BlockSpec((None, nb, D_ATT), lambda i, j: (i, 0, 0)),
                  pl.BlockSpec(t5.shape, lambda i, j: (0, 0, 0, 0))],
        out_specs=pl.BlockSpec((tq, D_ATT), lambda i, j: (i * nq + j, 0)),
        out_shape=jax.ShapeDtypeStruct((b * t, D_ATT), f32),
        compiler_params=_cp(("parallel", "arbitrary")), name="moba_prompt")(rel_bias, qkv, qkv, vt, kmean, t5)


def _outproj_body(x_ref, a_ref, f_ref, m_ref, w_ref, o_ref):
    acc = _dot(a_ref[...].astype(bf16), w_ref[0:D_CONV, :])
    acc = acc + _dot(f_ref[...].astype(bf16), w_ref[D_CONV:D_CONV + D_ATT, :])
    acc = acc + _dot(m_ref[...].astype(bf16), w_ref[D_CONV + D_ATT:, :])
    o_ref[...] = x_ref[...] + acc


def _outproj(x, a, fo, mo, w):
    m = x.shape[0]
    tm = min(ROW_TILE, m)
    row = lambda wd: pl.BlockSpec((tm, wd), lambda i: (i, 0))
    return pl.pallas_call(
        _outproj_body, grid=(m // tm,),
        in_specs=[row(D_MODEL), row(D_CONV), row(D_ATT), row(D_ATT), pl.BlockSpec(w.shape, lambda i: (0, 0))],
        out_specs=row(D_MODEL), out_shape=jax.ShapeDtypeStruct(x.shape, f32),
        compiler_params=_cp(("parallel",)), name="outproj")(x, a, fo, mo, w)


def _router_body(x_ref, g_ref, w_ref, o_ref):
    logits = _dot(_rms(x_ref[...], g_ref[...]).astype(bf16), w_ref[...])
    lane = lax.broadcasted_iota(jnp.int32, logits.shape, 1)
    g = jnp.where(lane < N_EXPERTS, logits, -jnp.inf)
    v1 = jnp.max(g, axis=-1, keepdims=True)
    i1 = jnp.min(jnp.where(g == v1, lane, LANES), axis=-1, keepdims=True)
    g2 = jnp.where(lane == i1, -jnp.inf, g)
    v2 = jnp.max(g2, axis=-1, keepdims=True)
    i2 = jnp.min(jnp.where(g2 == v2, lane, LANES), axis=-1, keepdims=True)
    e2 = jnp.exp(v2 - v1)
    w1 = 1.0 / (1.0 + e2)
    w2 = e2 / (1.0 + e2)
    o_ref[...] = jnp.where(lane == i1, w1, 0.0) + jnp.where(lane == i2, w2, 0.0)


def _router(x, g, w_pad):
    m = x.shape[0]
    tm = min(ROW_TILE, m)
    return pl.pallas_call(
        _router_body, grid=(m // tm,),
        in_specs=[pl.BlockSpec((tm, D_MODEL), lambda i: (i, 0)), pl.BlockSpec(g.shape, lambda i: (0, 0)),
                  pl.BlockSpec(w_pad.shape, lambda i: (0, 0))],
        out_specs=pl.BlockSpec((tm, LANES), lambda i: (i, 0)),
        out_shape=jax.ShapeDtypeStruct((m, LANES), f32),
        compiler_params=_cp(("parallel",)), name="router")(x, g, w_pad)


def _ffn_body(has_comb, has_final, x_ref, g_ref, *rest):
    rest = list(rest)
    comb_ref = rest.pop(0) if has_comb else None
    wg_ref, wu_ref, wd_ref = rest[:3]
    rest = rest[3:]
    gfin_ref = rest.pop(0) if has_final else None
    o_ref, hn_s, acc_s = rest
    e, j = pl.program_id(1), pl.program_id(2)
    first = (e == 0) & (j == 0)
    last = (e == pl.num_programs(1) - 1) & (j == pl.num_programs(2) - 1)

    @pl.when(first)
    def _():
        hn_s[...] = _rms(x_ref[...], g_ref[...]).astype(bf16)
        acc_s[...] = jnp.zeros_like(acc_s)

    hn = hn_s[...]
    h = _silu(_dot(hn, wg_ref[...].astype(bf16))) * _dot(hn, wu_ref[...].astype(bf16))
    if has_comb:
        comb = comb_ref[...]
        lane = lax.broadcasted_iota(jnp.int32, comb.shape, 1)
        h = h * jnp.sum(jnp.where(lane == e, comb, 0.0), axis=-1, keepdims=True)
    acc_s[...] += _dot(h.astype(bf16), wd_ref[...].astype(bf16))

    @pl.when(last)
    def _():
        y = x_ref[...] + acc_s[...]
        o_ref[...] = _rms(y, gfin_ref[...]) if has_final else y


def _ffn(x, g, comb, wg, wu, wd, gfin):
    m = x.shape[0]
    n_e, _, n_f = wg.shape
    tm = min(FFN_ROW_TILE, m)
    tf = FFN_COL_TILE
    in_specs = [pl.BlockSpec((tm, D_MODEL), lambda i, e, j: (i, 0)), pl.BlockSpec(g.shape, lambda i, e, j: (0, 0))]
    args = [x, g]
    if comb is not None:
        in_specs.append(pl.BlockSpec((tm, LANES), lambda i, e, j: (i, 0)))
        args.append(comb)
    in_specs += [pl.BlockSpec((None, D_MODEL, tf), lambda i, e, j: (e, 0, j)),
                 pl.BlockSpec((None, D_MODEL, tf), lambda i, e, j: (e, 0, j)),
                 pl.BlockSpec((None, tf, D_MODEL), lambda i, e, j: (e, j, 0))]
    args += [wg, wu, wd]
    if gfin is not None:
        in_specs.append(pl.BlockSpec(gfin.shape, lambda i, e, j: (0, 0)))
        args.append(gfin)
    return pl.pallas_call(
        functools.partial(_ffn_body, comb is not None, gfin is not None),
        grid=(m // tm, n_e, n_f // tf),
        in_specs=in_specs,
        out_specs=pl.BlockSpec((tm, D_MODEL), lambda i, e, j: (i, 0)),
        out_shape=jax.ShapeDtypeStruct(x.shape, f32),
        scratch_shapes=[pltpu.VMEM((tm, D_MODEL), bf16), pltpu.VMEM((tm, D_MODEL), f32)],
        compiler_params=_cp(("parallel", "arbitrary", "arbitrary")), name="ffn")(*args)


def _page_spec(layer, slot, chunk=lambda c: c):
    return pl.BlockSpec((None, None, N_HEADS, HEAD_DIM, PAGE_SIZE),
                        lambda b, c, pt: (layer, pt[b, chunk(c) * DEC_PAGES + slot], 0, 0, 0))


def _logf_gather_body(pt_ref, *refs):
    o_ref = refs[-1]
    b, c = pl.program_id(0), pl.program_id(1)
    for s in range(DEC_PAGES):
        row = pt_ref[b, c * DEC_PAGES + s] % SUBLANES
        o_ref[:, pl.ds(s, 1), :] = refs[s][:, pl.ds(row, 1), :]


def _logf_gather(page_table, logf_t, layer):
    b, n_pages = page_table.shape

    def spec(slot):
        return pl.BlockSpec((None, N_HEADS, SUBLANES, PAGE_SIZE),
                            lambda i, c, pt: (layer, 0, pt[i, c * DEC_PAGES + slot] // SUBLANES, 0))

    return pl.pallas_call(
        _logf_gather_body,
        grid_spec=pltpu.PrefetchScalarGridSpec(
            num_scalar_prefetch=1, grid=(b, n_pages // DEC_PAGES),
            in_specs=[spec(s) for s in range(DEC_PAGES)],
            out_specs=pl.BlockSpec((None, N_HEADS, DEC_PAGES, PAGE_SIZE), lambda i, c, pt: (i, 0, c, 0))),
        out_shape=jax.ShapeDtypeStruct((b, N_HEADS, n_pages, PAGE_SIZE), f32),
        compiler_params=_cp(("parallel", "arbitrary")), name="logf_gather")(page_table, *([logf_t] * DEC_PAGES))


def _suffix_body(lf_ref, new_ref, o_ref):
    x = lf_ref[...]
    n = x.shape[1]
    pos = lax.broadcasted_iota(jnp.int32, x.shape, 1)
    y = x
    s = 1
    while s < n:
        y = y + jnp.where(pos + s < n, pltpu.roll(y, n - s, axis=1), 0.0)
        s *= 2
    o_ref[...] = (y - x) + new_ref[:, 0:1]


def _suffix_bias(lf_t, lf_new):
    b, _, n = lf_t.shape
    return pl.pallas_call(
        _suffix_body, grid=(b,),
        in_specs=[pl.BlockSpec((None, SUBLANES, n), lambda i: (i, 0, 0)),
                  pl.BlockSpec((None, SUBLANES, LANES), lambda i: (i, 0, 0))],
        out_specs=pl.BlockSpec((None, SUBLANES, n), lambda i: (i, 0, 0)),
        out_shape=jax.ShapeDtypeStruct(lf_t.shape, f32),
        compiler_params=_cp(("parallel",)), name="suffix_bias")(lf_t, lf_new)


def _block_diag(x):
    b = x.shape[0]
    eye = jnp.eye(SUBLANES, N_HEADS, dtype=x.dtype)
    return (eye[None, :, :, None] * x[:, None, :, :]).reshape(b, SUBLANES, D_ATT)


def _rows_of_head(x):
    eye = jnp.eye(N_HEADS, SUBLANES, dtype=x.dtype)
    return eye[None, :, :, None] * x[:, :, None, :]


def _flat_page(ref):
    return ref[...].reshape(D_ATT, PAGE_SIZE).astype(bf16)


def _fox_dec_body(pt_ref, q_ref, kn_ref, vn_ref, bias_ref, *refs):
    k_refs, v_refs = refs[:DEC_PAGES], refs[DEC_PAGES:2 * DEC_PAGES]
    o_ref, s_all, m_s, inv_s, acc_s = refs[2 * DEC_PAGES:]
    nc = s_all.shape[0]
    c = pl.program_id(1)
    q = q_ref[...]

    @pl.when(c == 0)
    def _():
        m_s[...] = jnp.full_like(m_s, -jnp.inf)

    @pl.when(c < nc)
    def _():
        qb = q.astype(bf16)
        s = jnp.concatenate([_dot(qb, _flat_page(k_refs[g])) for g in range(DEC_PAGES)], axis=1) + bias_ref[...]
        s_all[c] = s
        m_s[...] = jnp.maximum(m_s[...], jnp.max(s, axis=-1, keepdims=True))

    @pl.when(c == nc)
    def _():
        s_new = jnp.sum(_round_bf16(q) * _round_bf16(kn_ref[...]), axis=-1, keepdims=True)
        m = jnp.maximum(m_s[...], s_new)
        p_new = jnp.exp(s_new - m)
        l = p_new
        for n in range(nc):
            p = jnp.exp(s_all[n] - m)
            s_all[n] = p
            l = l + jnp.sum(p, axis=-1, keepdims=True)
        inv = 1.0 / l
        inv_s[...] = inv
        acc_s[...] = _round_bf16(p_new * inv) * _round_bf16(vn_ref[...])

    @pl.when(c >= nc)
    def _():
        p = (s_all[c - nc] * inv_s[...]).astype(bf16)
        pv = acc_s[...]
        for g in range(DEC_PAGES):
            pv = pv + _dot_nt(p[:, g * PAGE_SIZE:(g + 1) * PAGE_SIZE], _flat_page(v_refs[g]))
        acc_s[...] = pv

    @pl.when(c == 2 * nc - 1)
    def _():
        row = lax.broadcasted_iota(jnp.int32, acc_s.shape, 0)
        col = lax.broadcasted_iota(jnp.int32, acc_s.shape, 1)
        own = (col >= row * HEAD_DIM) & (col < (row + 1) * HEAD_DIM)
        o_ref[...] = jnp.sum(jnp.where(own, acc_s[...], 0.0), axis=0, keepdims=True)


def _fox_decode(page_table, q, k_new, v_new, bias, cache_k, cache_v, layer):
    b, n_pages = page_table.shape
    nc = n_pages // DEC_PAGES
    small = pl.BlockSpec((None, SUBLANES, D_ATT), lambda i, c, pt: (i, 0, 0))
    k_chunk = lambda c: jnp.minimum(c, nc - 1)
    v_chunk = lambda c: jnp.maximum(c - nc, 0)
    return pl.pallas_call(
        _fox_dec_body,
        grid_spec=pltpu.PrefetchScalarGridSpec(
            num_scalar_prefetch=1, grid=(b, 2 * nc),
            in_specs=[small, small, small,
                      pl.BlockSpec((None, SUBLANES, DEC_PAGES * PAGE_SIZE), lambda i, c, pt: (i, 0, k_chunk(c)))]
            + [_page_spec(layer, s, k_chunk) for s in range(DEC_PAGES)]
            + [_page_spec(layer, s, v_chunk) for s in range(DEC_PAGES)],
            out_specs=pl.BlockSpec((None, 1, D_ATT), lambda i, c, pt: (i, 0, 0)),
            scratch_shapes=[pltpu.VMEM((nc, SUBLANES, DEC_PAGES * PAGE_SIZE), f32), pltpu.VMEM((SUBLANES, 1), f32),
                            pltpu.VMEM((SUBLANES, 1), f32), pltpu.VMEM((SUBLANES, D_ATT), f32)]),
        out_shape=jax.ShapeDtypeStruct((b, 1, D_ATT), f32),
        compiler_params=_cp(("parallel", "arbitrary")), name="fox_decode")(
            page_table, q, k_new, v_new, bias, *([cache_k] * DEC_PAGES), *([cache_v] * DEC_PAGES))


def _gate_body(pt_ref, q_ref, *refs):
    o_ref = refs[-1]
    pages_per_block = MOBA_BLOCK // PAGE_SIZE
    q = _round_bf16(q_ref[...])
    for blk in range(DEC_PAGES // pages_per_block):
        ksum = refs[blk * pages_per_block][...]
        for s in range(1, pages_per_block):
            ksum = ksum + refs[blk * pages_per_block + s][...]
        kmean = jnp.sum(ksum, axis=-1, keepdims=True) * (1.0 / MOBA_BLOCK)
        o_ref[blk] = jnp.sum(_round_bf16(kmean) * q, axis=1)


def _gate_decode(page_table, q_col, cache_k, layer):
    b, n_pages = page_table.shape
    per_step = DEC_PAGES * PAGE_SIZE // MOBA_BLOCK
    return pl.pallas_call(
        _gate_body,
        grid_spec=pltpu.PrefetchScalarGridSpec(
            num_scalar_prefetch=1, grid=(b, n_pages // DEC_PAGES),
            in_specs=[pl.BlockSpec((None, N_HEADS, HEAD_DIM, 1), lambda i, c, pt: (i, 0, 0, 0))]
            + [_page_spec(layer, s) for s in range(DEC_PAGES)],
            out_specs=pl.BlockSpec((None, per_step, N_HEADS, 1), lambda i, c, pt: (i, c, 0, 0))),
        out_shape=jax.ShapeDtypeStruct((b, n_pages * PAGE_SIZE // MOBA_BLOCK, N_HEADS, 1), f32),
        compiler_params=_cp(("parallel", "arbitrary")), name="gate_decode")(page_table, q_col, *([cache_k] * DEC_PAGES))


def _select_body(g_ref, o_ref):
    g = g_ref[...]
    nb = g.shape[0]
    blk = lax.broadcasted_iota(jnp.int32, g.shape, 0)
    for t in range(MOBA_TOPK):
        mx = jnp.max(g, axis=0, keepdims=True)
        idx = jnp.min(jnp.where(g == mx, blk, nb), axis=0, keepdims=True)
        o_ref[t] = idx[0]
        g = jnp.where(blk == idx, -jnp.inf, g)


def _select_decode(gates):
    b, nb = gates.shape[:2]
    return pl.pallas_call(
        _select_body, grid=(b,),
        in_specs=[pl.BlockSpec((None, nb, N_HEADS, 1), lambda i: (i, 0, 0, 0))],
        out_specs=pl.BlockSpec((None, MOBA_TOPK, N_HEADS, 1), lambda i: (i, 0, 0, 0)),
        out_shape=jax.ShapeDtypeStruct((b, MOBA_TOPK, N_HEADS, 1), jnp.int32),
        compiler_params=_cp(("parallel",)), name="select_decode")(gates)


def _moba_dec_body(past_len, pt_ref, idx_ref, q_ref, kn_ref, vn_ref, tab_ref, *refs):
    pages_per_block = MOBA_BLOCK // PAGE_SIZE
    n_sel = MOBA_TOPK * pages_per_block
    k_refs, v_refs, o_ref = refs[:n_sel], refs[n_sel:2 * n_sel], refs[2 * n_sel]
    b, h = pl.program_id(0), pl.program_id(1)
    q = q_ref[...]
    qb = q.astype(bf16)
    tab = tab_ref[...]
    lane = lax.broadcasted_iota(jnp.int32, (1, PAGE_SIZE), 1)

    def t5(dist):
        bucket = _t5_bucket(dist)
        out = jnp.zeros((SUBLANES, dist.shape[1]), f32)
        for nbk in range(N_BUCKETS):
            out = jnp.where(bucket == nbk, tab[:, nbk:nbk + 1], out)
        return out

    scores = []
    for t in range(MOBA_TOPK):
        blk = idx_ref[b, t, h]
        for s in range(pages_per_block):
            kpos = blk * MOBA_BLOCK + s * PAGE_SIZE + lane
            scores.append(_dot(qb, k_refs[t * pages_per_block + s][...].astype(bf16)) + t5(past_len - kpos))
    s_past = jnp.concatenate(scores, axis=1)
    s_new = (jnp.sum(_round_bf16(q) * _round_bf16(kn_ref[...]), axis=-1, keepdims=True)
             + t5(jnp.zeros((1, 1), jnp.int32)))
    m = jnp.maximum(jnp.max(s_past, axis=-1, keepdims=True), s_new)
    p = jnp.exp(s_past - m)
    p_new = jnp.exp(s_new - m)
    inv = 1.0 / (jnp.sum(p, axis=-1, keepdims=True) + p_new)
    p = (p * inv).astype(bf16)
    acc = _round_bf16(p_new * inv) * _round_bf16(vn_ref[...])
    for g in range(n_sel):
        acc = acc + _dot_nt(p[:, g * PAGE_SIZE:(g + 1) * PAGE_SIZE], v_refs[g][...].astype(bf16))
    row = lax.broadcasted_iota(jnp.int32, (SUBLANES, HEAD_DIM), 0)
    res = jnp.where(row == h, acc, 0.0)

    @pl.when(h == 0)
    def _():
        o_ref[...] = res

    @pl.when(h > 0)
    def _():
        o_ref[...] += res


def _moba_decode(page_table, idx, q, k_new, v_new, tab_t, cache_k, cache_v, layer):
    b, n_pages = page_table.shape
    pages_per_block = MOBA_BLOCK // PAGE_SIZE
    small = pl.BlockSpec((None, None, SUBLANES, HEAD_DIM), lambda i, h, pt, ix: (i, h, 0, 0))

    def sel_spec(t, s):
        return pl.BlockSpec((None, None, None, HEAD_DIM, PAGE_SIZE),
                            lambda i, h, pt, ix: (layer, pt[i, ix[i, t, h] * pages_per_block + s], h, 0, 0))

    specs = [sel_spec(t, s) for t in range(MOBA_TOPK) for s in range(pages_per_block)]
    return pl.pallas_call(
        functools.partial(_moba_dec_body, n_pages * PAGE_SIZE),
        grid_spec=pltpu.PrefetchScalarGridSpec(
            num_scalar_prefetch=2, grid=(b, N_HEADS),
            in_specs=[small, small, small, pl.BlockSpec(tab_t.shape, lambda i, h, pt, ix: (0, 0))] + specs * 2,
            out_specs=pl.BlockSpec((None, SUBLANES, HEAD_DIM), lambda i, h, pt, ix: (i, 0, 0))),
        out_shape=jax.ShapeDtypeStruct((b, SUBLANES, HEAD_DIM), f32),
        compiler_params=_cp(("parallel", "arbitrary")), name="moba_decode")(
            page_table, idx, q, k_new, v_new, tab_t, *([cache_k] * len(specs)), *([cache_v] * len(specs)))


def _split_w_in(w):
    o_f = 2 * D_CONV
    o_g = o_f + 3 * D_ATT
    o_m = o_g + N_HEADS
    wg = jnp.zeros((D_MODEL, LANES), f32).at[:, :N_HEADS].set(w[:, o_g:o_m])
    return w[:, :o_f].astype(bf16), w[:, o_f:o_g].astype(bf16), w[:, o_m:].astype(bf16), wg.astype(bf16)


def _pad_lanes(v):
    return jnp.zeros((1, LANES), f32).at[0, :v.shape[0]].set(v)


def _layer_weights(l, p):
    wglu, wf, wm, wg = _split_w_in(p["w_in"][l])
    i = l // 2
    d = dict(norm_mix=p["norm_mix"][l][None], norm_ffn=p["norm_ffn"][l][None],
             wglu=wglu, wf=wf, wm=wm, wg=wg, bfg=_pad_lanes(p["b_forget"][l]),
             conv_w=p["conv_w"][l], conv_b=p["conv_b"][l][None], ln_g=p["conv_ln_g"][l][None], ln_b=p["conv_ln_b"][l][None],
             w_pw=p["w_conv_pw"][l].astype(bf16), w_out=p["w_out"][l].astype(bf16))
    if l % 2 == 0:
        d.update(ffn_g=p["w_dense_gate"][i][None], ffn_u=p["w_dense_up"][i][None], ffn_d=p["w_dense_down"][i][None], router=None)
    else:
        d.update(ffn_g=p["w_moe_gate"][i], ffn_u=p["w_moe_up"][i], ffn_d=p["w_moe_down"][i],
                 router=jnp.zeros((D_MODEL, LANES), f32).at[:, :N_EXPERTS].set(p["w_router"][i]).astype(bf16))
    return d


def _channel_mix(x, w, gfin):
    comb = None if w["router"] is None else _router(x, w["norm_ffn"], w["router"])
    return _ffn(x, w["norm_ffn"], comb, w["ffn_g"], w["ffn_u"], w["ffn_d"], gfin)


def _heads(a, b, t):
    return a.reshape(b, t, N_HEADS, HEAD_DIM)


def _prompt_layer(x, w, rel_bias, t5, b, t, gfin):
    u, ff, fm, bff, bfm, lf, lft, fvt, mvt = _proj(x, w["norm_mix"], w["wglu"], w["wf"], w["wm"], w["wg"], w["bfg"], True)
    u3 = u.reshape(b, t, D_CONV)
    up = jnp.concatenate([jnp.zeros((b, CONV_HALO, D_CONV), f32), u3], axis=1)
    conv_out = _conv_prompt(up, w["conv_w"], w["conv_b"], w["ln_g"], w["ln_b"], w["w_pw"], t).reshape(b * t, D_CONV)
    ccol, crow = _cumsum(lf, lft, b, t)
    nq = t // min(ATT_TILE, t)
    crow3 = crow.reshape(SUBLANES, b * nq, t // nq).transpose(1, 0, 2)
    fox_out = _fox_prompt(bff, fvt, ccol, crow3, b, t)
    kmean = _kmean_prompt(fm, b, t)
    moba_out = _moba_prompt(rel_bias, bfm, mvt, kmean, t5, b, t)
    x = _outproj(x, conv_out, fox_out, moba_out, w["w_out"])
    x = _channel_mix(x, w, gfin)
    state = (_heads(ff[:, D_ATT:2 * D_ATT], b, t), _heads(ff[:, 2 * D_ATT:], b, t), lf[:, :N_HEADS].reshape(b, t, N_HEADS),
             _heads(fm[:, D_ATT:2 * D_ATT], b, t), _heads(fm[:, 2 * D_ATT:], b, t), u3[:, t - (CONV_WIDTH - 1):])
    return x, state


def _sample_layer(x, w, l, rel_bias, caches, page_table, gfin):
    cache_fk, cache_fv, logf_t, cache_mk, cache_mv, state_t = caches
    b = x.shape[0]
    n_past = page_table.shape[1] * PAGE_SIZE
    u, ff, fm, _, _, lf = _proj(x, w["norm_mix"], w["wglu"], w["wf"], w["wm"], w["wg"], w["bfg"], False)
    conv_out = _conv_step(state_t[l], u, w["conv_w"], w["conv_b"], w["ln_g"], w["ln_b"], w["w_pw"])
    heads = lambda a: a.reshape(b, N_HEADS, HEAD_DIM)
    fq, fk, fv = heads(ff[:, :D_ATT]) * Q_SCALE, heads(ff[:, D_ATT:2 * D_ATT]), heads(ff[:, 2 * D_ATT:])
    mq, mk, mv = heads(fm[:, :D_ATT]) * Q_SCALE, heads(fm[:, D_ATT:2 * D_ATT]), heads(fm[:, 2 * D_ATT:])
    lf_pages = _logf_gather(page_table, logf_t, l)
    lf_t = jnp.pad(lf_pages.reshape(b, N_HEADS, n_past), ((0, 0), (0, SUBLANES - N_HEADS), (0, 0)))
    lf_new = jnp.broadcast_to(jnp.pad(lf[:, :N_HEADS], ((0, 0), (0, SUBLANES - N_HEADS)))[:, :, None], (b, SUBLANES, LANES))
    bias = _suffix_bias(lf_t, lf_new)
    fox_out = _fox_decode(page_table, _block_diag(fq), _block_diag(fk), _block_diag(fv), bias,
                          cache_fk, cache_fv, l).reshape(b, D_ATT)
    idx = _select_decode(_gate_decode(page_table, mq[..., None], cache_mk, l))[..., 0]
    tab_t = jnp.pad(rel_bias.T, ((0, SUBLANES - N_HEADS), (0, 0)))
    moba_out = _moba_decode(page_table, idx, _rows_of_head(mq), _rows_of_head(mk), _rows_of_head(mv), tab_t,
                            cache_mk, cache_mv, l)[:, :N_HEADS].reshape(b, D_ATT)
    x = _outproj(x, conv_out, fox_out, moba_out, w["w_out"])
    x = _channel_mix(x, w, gfin)
    conv_state_t = jnp.concatenate([state_t[l][1:], u[None]], axis=0)
    state = (fk[:, None], fv[:, None], lf[:, None, :N_HEADS], mk[:, None], mv[:, None], conv_state_t)
    return x, state


def kernel(x_prompt, x_sample, cache_fox_k, cache_fox_v, cache_fox_logf, cache_moba_k, cache_moba_v, state_conv, page_table, norm_mix, norm_ffn, norm_final, w_in, b_forget, conv_w, conv_b, conv_ln_g, conv_ln_b, w_conv_pw, rel_bias, w_out, w_dense_gate, w_dense_up, w_dense_down, w_router, w_moe_gate, w_moe_up, w_moe_down):
    p = dict(norm_mix=norm_mix, norm_ffn=norm_ffn, w_in=w_in, b_forget=b_forget, conv_w=conv_w, conv_b=conv_b,
             conv_ln_g=conv_ln_g, conv_ln_b=conv_ln_b, w_conv_pw=w_conv_pw, w_out=w_out, w_dense_gate=w_dense_gate,
             w_dense_up=w_dense_up, w_dense_down=w_dense_down, w_router=w_router, w_moe_gate=w_moe_gate,
             w_moe_up=w_moe_up, w_moe_down=w_moe_down)
    depth = norm_mix.shape[0]
    weights = [_layer_weights(l, p) for l in range(depth)]
    gfin = norm_final[None]
    t5 = _t5_tiles(rel_bias)

    b, t, _ = x_prompt.shape
    x = x_prompt.reshape(b * t, D_MODEL)
    states_p = []
    for l in range(depth):
        x, st = _prompt_layer(x, weights[l], rel_bias, t5, b, t, gfin if l == depth - 1 else None)
        states_p.append(st)
    y_prompt = x.reshape(b, t, D_MODEL)

    bs = x_sample.shape[0]
    rows_on_lanes = lambda c: jnp.transpose(c, (0, 1, 3, 4, 2))
    caches = (rows_on_lanes(cache_fox_k), rows_on_lanes(cache_fox_v), jnp.transpose(cache_fox_logf, (0, 3, 1, 2)),
              rows_on_lanes(cache_moba_k), rows_on_lanes(cache_moba_v), jnp.transpose(state_conv, (0, 2, 1, 3)))
    x = x_sample.reshape(bs, D_MODEL)
    states_s = []
    for l in range(depth):
        x, st = _sample_layer(x, weights[l], l, rel_bias, caches, page_table, gfin if l == depth - 1 else None)
        states_s.append(st)
    y_sample = x.reshape(bs, 1, D_MODEL)

    stack = lambda sts: [jnp.stack(s) for s in zip(*sts)]
    out_s = stack(states_s)
    out_s[-1] = jnp.transpose(out_s[-1], (0, 2, 1, 3))
    return (y_prompt, y_sample, *stack(states_p), *out_s)
```

```python
import functools
import math

import jax
import jax.numpy as jnp
from jax import lax
from jax.experimental import pallas as pl
from jax.experimental.pallas import tpu as pltpu

f32 = jnp.float32
bf16 = jnp.bfloat16

D_MODEL = 1024
HEAD_DIM = 64
N_HEADS = 6
D_ATT = N_HEADS * HEAD_DIM
D_CONV = 256
CONV_WIDTH = 31
PAGE_SIZE = 128
MOBA_BLOCK = 256
MOBA_TOPK = 3
N_BUCKETS = 32
MAX_DISTANCE = 128
N_EXPERTS = 8
Q_SCALE = HEAD_DIM ** -0.5

LANES = 128
SUBLANES = 8
VMEM_LIMIT = 56 * 1024 * 1024

ROW_TILE = 256
ATT_TILE = 256
ATT_HEAD_GROUP = 6
CONV_TILE = 512
CONV_HALO = 32
FFN_ROW_TILE = 1024
FFN_COL_TILE = 256
MOE_CHUNK = 256
DEC_PAGES = 16


def _cp(sem):
    return pltpu.CompilerParams(dimension_semantics=sem, vmem_limit_bytes=VMEM_LIMIT)


def _rms(x, g):
    return (x * lax.rsqrt(jnp.mean(x * x, axis=-1, keepdims=True) + 1e-6)) * g


def _log_sigmoid(x):
    return jnp.minimum(x, 0.0) - jnp.log(1.0 + jnp.exp(-jnp.abs(x)))


def _silu(x):
    return x * jax.nn.sigmoid(x)


def _round_bf16(x):
    return x.astype(bf16).astype(f32)


def _dot(a, b):
    return jnp.dot(a, b, preferred_element_type=f32)


def _dot_nt(a, b, precision=None):
    return lax.dot_general(a, b, (((1,), (1,)), ((), ())), preferred_element_type=f32, precision=precision)


def _t5_bucket(d):
    max_exact = N_BUCKETS // 2
    df = jnp.maximum(d, max_exact).astype(f32)
    large = max_exact + (jnp.log(df / max_exact) / math.log(MAX_DISTANCE / max_exact)
                         * (N_BUCKETS - max_exact)).astype(jnp.int32)
    return jnp.where(d < max_exact, d, jnp.minimum(large, N_BUCKETS - 1))


def _proj_body(with_t, x_ref, g_ref, wglu_ref, wf_ref, wm_ref, wg_ref, bfg_ref,
               u_ref, ff_ref, fm_ref, bff_ref, bfm_ref, lf_ref, *maybe_lft):
    hb = _rms(x_ref[...], g_ref[...]).astype(bf16)
    glu = _dot(hb, wglu_ref[...])
    u_ref[...] = glu[:, :D_CONV] * jax.nn.sigmoid(glu[:, D_CONV:])
    col = lax.broadcasted_iota(jnp.int32, (1, 3 * D_ATT), 1)
    qscale = jnp.where(col < D_ATT, Q_SCALE, 1.0).astype(f32)
    pf = _dot(hb, wf_ref[...])
    ff_ref[...] = pf
    bff_ref[...] = (pf * qscale).astype(bf16)
    pm = _dot(hb, wm_ref[...])
    fm_ref[...] = pm
    bfm_ref[...] = (pm * qscale).astype(bf16)
    lf = _log_sigmoid(_dot(hb, wg_ref[...]) + bfg_ref[...])
    lf_ref[...] = lf
    if with_t:
        lft_ref, fvt_ref, mvt_ref = maybe_lft
        lft_ref[...] = lf.T[:SUBLANES]
        fvt_ref[...] = pf[:, 2 * D_ATT:].T.astype(bf16)
        mvt_ref[...] = pm[:, 2 * D_ATT:].T.astype(bf16)


def _proj(x, g, wglu, wf, wm, wg, bfg, with_t):
    m = x.shape[0]
    tm = min(ROW_TILE, m)
    row = lambda w: pl.BlockSpec((tm, w), lambda i: (i, 0))
    full = lambda a: pl.BlockSpec(a.shape, lambda i: (0, 0))
    out_shape = [jax.ShapeDtypeStruct((m, D_CONV), f32),
                 jax.ShapeDtypeStruct((m, 3 * D_ATT), f32), jax.ShapeDtypeStruct((m, 3 * D_ATT), f32),
                 jax.ShapeDtypeStruct((m, 3 * D_ATT), bf16), jax.ShapeDtypeStruct((m, 3 * D_ATT), bf16),
                 jax.ShapeDtypeStruct((m, LANES), f32)]
    out_specs = [row(D_CONV), row(3 * D_ATT), row(3 * D_ATT), row(3 * D_ATT), row(3 * D_ATT), row(LANES)]
    if with_t:
        out_shape.append(jax.ShapeDtypeStruct((SUBLANES, m), f32))
        out_specs.append(pl.BlockSpec((SUBLANES, tm), lambda i: (0, i)))
        for _ in range(2):
            out_shape.append(jax.ShapeDtypeStruct((m // tm, D_ATT, tm), bf16))
            out_specs.append(pl.BlockSpec((None, D_ATT, tm), lambda i: (i, 0, 0)))
    return pl.pallas_call(
        functools.partial(_proj_body, with_t),
        grid=(m // tm,),
        in_specs=[row(D_MODEL), full(g), full(wglu), full(wf), full(wm), full(wg), full(bfg)],
        out_specs=out_specs, out_shape=out_shape,
        compiler_params=_cp(("parallel",)), name="proj")(x, g, wglu, wf, wm, wg, bfg)


def _cumsum_body(lf_ref, lft_ref, ccol_ref, crow_ref):
    x = lf_ref[...]
    t = x.shape[0]
    pos = lax.broadcasted_iota(jnp.int32, x.shape, 0)
    s = 1
    while s < t:
        x = x + jnp.where(pos >= s, pltpu.roll(x, s, axis=0), 0.0)
        s *= 2
    ccol_ref[...] = x
    y = lft_ref[...]
    pos = lax.broadcasted_iota(jnp.int32, y.shape, 1)
    s = 1
    while s < t:
        y = y + jnp.where(pos >= s, pltpu.roll(y, s, axis=1), 0.0)
        s *= 2
    crow_ref[...] = y


def _cumsum(lf, lft, b, t):
    return pl.pallas_call(
        _cumsum_body, grid=(b,),
        in_specs=[pl.BlockSpec((t, LANES), lambda i: (i, 0)), pl.BlockSpec((SUBLANES, t), lambda i: (0, i))],
        out_specs=[pl.BlockSpec((t, LANES), lambda i: (i, 0)), pl.BlockSpec((SUBLANES, t), lambda i: (0, i))],
        out_shape=[jax.ShapeDtypeStruct(lf.shape, f32), jax.ShapeDtypeStruct(lft.shape, f32)],
        compiler_params=_cp(("parallel",)), name="cumsum")(lf, lft)


def _conv_tail(y, lng_ref, lnb_ref, wpw_ref):
    mu = jnp.mean(y, axis=-1, keepdims=True)
    var = jnp.mean(jnp.square(y - mu), axis=-1, keepdims=True)
    z = _silu((y - mu) * lax.rsqrt(var + 1e-5) * lng_ref[...] + lnb_ref[...])
    return _dot(z.astype(bf16), wpw_ref[...])


def _conv_body(main_ref, halo_ref, w_ref, b_ref, lng_ref, lnb_ref, wpw_ref, o_ref):
    tt = main_ref.shape[0]
    win = _round_bf16(jnp.concatenate([main_ref[...], halo_ref[...]], axis=0))
    w = _round_bf16(w_ref[...])
    lead = CONV_HALO - (CONV_WIDTH - 1)
    acc = jnp.zeros((tt, D_CONV), f32) + b_ref[...]
    for k in range(CONV_WIDTH):
        acc = acc + win[lead + k:lead + k + tt] * w[k:k + 1]
    o_ref[...] = _conv_tail(acc, lng_ref, lnb_ref, wpw_ref)


def _conv_prompt(up, w, b, lng, lnb, wpw, t):
    bsz = up.shape[0]
    tt = min(CONV_TILE, t)
    full = lambda a: pl.BlockSpec(a.shape, lambda i, j: (0, 0))
    return pl.pallas_call(
        _conv_body, grid=(bsz, t // tt),
        in_specs=[pl.BlockSpec((None, tt, D_CONV), lambda i, j: (i, j, 0)),
                  pl.BlockSpec((None, CONV_HALO, D_CONV), lambda i, j: (i, (j + 1) * (tt // CONV_HALO), 0)),
                  full(w), full(b), full(lng), full(lnb), full(wpw)],
        out_specs=pl.BlockSpec((None, tt, D_CONV), lambda i, j: (i, j, 0)),
        out_shape=jax.ShapeDtypeStruct((bsz, t, D_CONV), f32),
        compiler_params=_cp(("parallel", "parallel")), name="conv_prompt")(up, up, w, b, lng, lnb, wpw)


def _conv_step_body(st_ref, u_ref, w_ref, b_ref, lng_ref, lnb_ref, wpw_ref, o_ref):
    w = w_ref[...]
    acc = u_ref[...] * w[CONV_WIDTH - 1:CONV_WIDTH] + b_ref[...]
    for k in range(CONV_WIDTH - 1):
        acc = acc + st_ref[k] * w[k:k + 1]
    o_ref[...] = _conv_tail(acc, lng_ref, lnb_ref, wpw_ref)


def _conv_step(state_t, u, w, b, lng, lnb, wpw):
    return pl.pallas_call(
        _conv_step_body, out_shape=jax.ShapeDtypeStruct(u.shape, f32), name="conv_step")(state_t, u, w, b, lng, lnb, wpw)


def _flash_step(qh, k, vt, bias_row, bias_col, bias_tile, mask, carry):
    m, l, acc = carry
    s = _dot_nt(k, qh)
    if bias_tile is not None:
        s = s + bias_tile
    if bias_col is not None:
        s = s - bias_col
    if bias_row is not None:
        s = s + bias_row
    if mask is not None:
        s = jnp.where(mask, s, -jnp.inf)
    m_new = jnp.maximum(m, jnp.max(s, axis=0, keepdims=True))
    alpha = jnp.exp(m - m_new)
    p = jnp.exp(s - m_new)
    l = alpha * l + jnp.sum(p, axis=0, keepdims=True)
    acc = alpha * acc + _dot(vt, p.astype(bf16))
    return m_new, l, acc


def _flash_init(tq):
    return (jnp.full((1, tq), -jnp.inf, f32), jnp.zeros((1, tq), f32), jnp.zeros((HEAD_DIM, tq), f32))


def _masked_queries(q_ref, h):
    j, hh = divmod(h, 2)
    lane = lax.broadcasted_iota(jnp.int32, (1, LANES), 1)
    qp = q_ref[:, j * LANES:(j + 1) * LANES]
    return jnp.where((lane >= hh * HEAD_DIM) & (lane < (hh + 1) * HEAD_DIM), qp, jnp.zeros_like(qp))


def _pair_cols(h):
    return slice((h // 2) * LANES, (h // 2 + 1) * LANES)


def _head_rows(h):
    return slice(h * HEAD_DIM, (h + 1) * HEAD_DIM)


def _fox_body(q_ref, k_ref, vt_ref, ccol_ref, crow_ref, o_ref):
    tq = q_ref.shape[0]
    i = pl.program_id(1)
    key = lax.broadcasted_iota(jnp.int32, (tq, tq), 0)
    qry = lax.broadcasted_iota(jnp.int32, (tq, tq), 1)
    causal = key <= qry
    outs = []
    for g0 in range(0, N_HEADS, ATT_HEAD_GROUP):
        group = range(g0, g0 + ATT_HEAD_GROUP)
        qhs = [_masked_queries(q_ref, h) for h in group]
        cqs = [crow_ref[i, h:h + 1, :] for h in group]

        def tile(n, carries, mask, group=group, qhs=qhs, cqs=cqs):
            rows = pl.ds(pl.multiple_of(n * tq, tq), tq)
            new = []
            for h, qh, cq, carry in zip(group, qhs, cqs, carries):
                ck = ccol_ref[rows, h:h + 1]
                new.append(_flash_step(qh, k_ref[rows, _pair_cols(h)], vt_ref[n, _head_rows(h), :], cq, ck, None, mask, carry))
            return tuple(new)

        init = tuple(_flash_init(tq) for _ in group)
        carries = lax.fori_loop(0, i, lambda n, cr, tile=tile: tile(n, cr, None), init)
        outs += [acc / l for _, l, acc in tile(i, carries, causal)]
    o_ref[...] = jnp.concatenate(outs, axis=0).T


def _fox_prompt(qkv, vt, ccol, crow3, b, t):
    tq = min(ATT_TILE, t)
    nq = t // tq
    return pl.pallas_call(
        _fox_body, grid=(b, nq),
        in_specs=[pl.BlockSpec((tq, D_ATT), lambda i, j: (i * nq + j, 0)),
                  pl.BlockSpec((t, D_ATT), lambda i, j: (i, 1)),
                  pl.BlockSpec((nq, D_ATT, tq), lambda i, j: (i, 0, 0)),
                  pl.BlockSpec((t, LANES), lambda i, j: (i, 0)),
                  pl.BlockSpec((nq, SUBLANES, tq), lambda i, j: (i, 0, 0))],
        out_specs=pl.BlockSpec((tq, D_ATT), lambda i, j: (i * nq + j, 0)),
        out_shape=jax.ShapeDtypeStruct((b * t, D_ATT), f32),
        compiler_params=_cp(("parallel", "arbitrary")), name="fox_prompt")(qkv, qkv, vt, ccol, crow3)


def _kmean_body(k_ref, o_ref):
    t = k_ref.shape[0]
    k = k_ref[...].reshape(t // MOBA_BLOCK, MOBA_BLOCK, D_ATT)
    o_ref[...] = jnp.mean(k, axis=1)


def _kmean_prompt(mqkv_f32, b, t):
    nb = t // MOBA_BLOCK
    return pl.pallas_call(
        _kmean_body, grid=(b,),
        in_specs=[pl.BlockSpec((t, D_ATT), lambda i: (i, 1))],
        out_specs=pl.BlockSpec((None, nb, D_ATT), lambda i: (i, 0, 0)),
        out_shape=jax.ShapeDtypeStruct((b, nb, D_ATT), f32),
        compiler_params=_cp(("parallel",)), name="kmean_prompt")(mqkv_f32)


def _t5_tiles_body(tab_ref, o_ref):
    h = pl.program_id(0)
    key = lax.broadcasted_iota(jnp.int32, (MOBA_BLOCK, MOBA_BLOCK), 0)
    qry = lax.broadcasted_iota(jnp.int32, (MOBA_BLOCK, MOBA_BLOCK), 1)
    for which in range(2):
        bucket = _t5_bucket(jnp.maximum(qry - key + which * MOBA_BLOCK, 0))
        acc = jnp.zeros((MOBA_BLOCK, MOBA_BLOCK), f32)
        for nbk in range(N_BUCKETS):
            acc = jnp.where(bucket == nbk, tab_ref[nbk, h], acc)
        o_ref[which] = acc


def _t5_tiles(rel_bias):
    return pl.pallas_call(
        _t5_tiles_body, grid=(N_HEADS,),
        in_specs=[pl.BlockSpec(memory_space=pltpu.SMEM)],
        out_specs=pl.BlockSpec((None, 2, MOBA_BLOCK, MOBA_BLOCK), lambda h: (h, 0, 0, 0)),
        out_shape=jax.ShapeDtypeStruct((N_HEADS, 2, MOBA_BLOCK, MOBA_BLOCK), f32),
        compiler_params=_cp(("parallel",)), name="t5_tiles")(rel_bias)


def _top_blocks(gate, n_valid):
    blk = lax.broadcasted_iota(jnp.int32, gate.shape, 0)
    nb = gate.shape[0]
    g = jnp.where(blk < n_valid, gate, -jnp.inf)
    sel = jnp.zeros(gate.shape, f32)
    for _ in range(min(MOBA_TOPK, nb)):
        mx = jnp.max(g, axis=0, keepdims=True)
        idx = jnp.min(jnp.where(g == mx, blk, nb), axis=0, keepdims=True)
        pick = (blk == idx) & (mx > -jnp.inf)
        sel = jnp.where(pick, 1.0, sel)
        g = jnp.where(blk == idx, -jnp.inf, g)
    return sel


def _moba_body(tab_ref, q_ref, k_ref, vt_ref, kmean_ref, t5_ref, o_ref):
    tq = q_ref.shape[0]
    nb = kmean_ref.shape[0]
    i = pl.program_id(1)
    key = lax.broadcasted_iota(jnp.int32, (tq, tq), 0)
    qry = lax.broadcasted_iota(jnp.int32, (tq, tq), 1)
    causal = key <= qry
    blk = lax.broadcasted_iota(jnp.int32, (nb, tq), 0)
    prev = jnp.maximum(i - 1, 0)
    outs = []
    for g0 in range(0, N_HEADS, ATT_HEAD_GROUP):
        group = range(g0, g0 + ATT_HEAD_GROUP)
        qhs = [_masked_queries(q_ref, h) for h in group]
        sels = [_top_blocks(_dot_nt(kmean_ref[:, _pair_cols(h)].astype(bf16), qh), i) for h, qh in zip(group, qhs)]

        def picked(n, sel):
            return jnp.max(jnp.where(blk == n, sel, 0.0), axis=0, keepdims=True) > 0.5

        def tile(n, which, own, carries, group=group, qhs=qhs, sels=sels):
            rows = pl.ds(pl.multiple_of(n * tq, tq), tq)
            new = []
            for h, qh, sel, carry in zip(group, qhs, sels, carries):
                bias_tile = None if which is None else t5_ref[h, which]
                bias_row = tab_ref[N_BUCKETS - 1, h] if which is None else None
                mask = causal if own else picked(n, sel)
                new.append(_flash_step(qh, k_ref[rows, _pair_cols(h)], vt_ref[n, _head_rows(h), :],
                                       bias_row, None, bias_tile, mask, carry))
            return tuple(new)

        carries = tile(i, 0, True, tuple(_flash_init(tq) for _ in group))
        carries = tile(prev, 1, False, carries)
        carries = lax.fori_loop(0, prev, lambda n, cr, tile=tile: tile(n, None, False, cr), carries)
        outs += [acc / l for _, l, acc in carries]
    o_ref[...] = jnp.concatenate(outs, axis=0).T


def _moba_prompt(rel_bias, qkv, vt, kmean, t5, b, t):
    tq = MOBA_BLOCK
    nq = t // tq
    nb = kmean.shape[1]
    return pl.pallas_call(
        _moba_body, grid=(b, nq),
        in_specs=[pl.BlockSpec(memory_space=pltpu.SMEM),
                  pl.BlockSpec((tq, D_ATT), lambda i, j: (i * nq + j, 0)),
                  pl.BlockSpec((t, D_ATT), lambda i, j: (i, 1)),
                  pl.BlockSpec((nq, D_ATT, tq), lambda i, j: (i, 0, 0)),
                  pl.BlockSpec((None, nb, D_ATT), lambda i, j: (i, 0, 0)),
                  pl.BlockSpec(t5.shape, lambda i, j: (0, 0, 0, 0))],
        out_specs=pl.BlockSpec((tq, D_ATT), lambda i, j: (i * nq + j, 0)),
        out_shape=jax.ShapeDtypeStruct((b * t, D_ATT), f32),
        compiler_params=_cp(("parallel", "arbitrary")), name="moba_prompt")(rel_bias, qkv, qkv, vt, kmean, t5)


def _outproj_body(x_ref, a_ref, f_ref, m_ref, w_ref, o_ref):
    acc = _dot(a_ref[...].astype(bf16), w_ref[0:D_CONV, :])
    acc = acc + _dot(f_ref[...].astype(bf16), w_ref[D_CONV:D_CONV + D_ATT, :])
    acc = acc + _dot(m_ref[...].astype(bf16), w_ref[D_CONV + D_ATT:, :])
    o_ref[...] = x_ref[...] + acc


def _outproj(x, a, fo, mo, w):
    m = x.shape[0]
    tm = min(ROW_TILE, m)
    row = lambda wd: pl.BlockSpec((tm, wd), lambda i: (i, 0))
    return pl.pallas_call(
        _outproj_body, grid=(m // tm,),
        in_specs=[row(D_MODEL), row(D_CONV), row(D_ATT), row(D_ATT), pl.BlockSpec(w.shape, lambda i: (0, 0))],
        out_specs=row(D_MODEL), out_shape=jax.ShapeDtypeStruct(x.shape, f32),
        compiler_params=_cp(("parallel",)), name="outproj")(x, a, fo, mo, w)


def _router_body(x_ref, g_ref, w_ref, o_ref):
    logits = _dot(_rms(x_ref[...], g_ref[...]).astype(bf16), w_ref[...])
    lane = lax.broadcasted_iota(jnp.int32, logits.shape, 1)
    g = jnp.where(lane < N_EXPERTS, logits, -jnp.inf)
    v1 = jnp.max(g, axis=-1, keepdims=True)
    i1 = jnp.min(jnp.where(g == v1, lane, LANES), axis=-1, keepdims=True)
    g2 = jnp.where(lane == i1, -jnp.inf, g)
    v2 = jnp.max(g2, axis=-1, keepdims=True)
    i2 = jnp.min(jnp.where(g2 == v2, lane, LANES), axis=-1, keepdims=True)
    e2 = jnp.exp(v2 - v1)
    w1 = 1.0 / (1.0 + e2)
    w2 = e2 / (1.0 + e2)
    o_ref[...] = jnp.where(lane == i1, w1, 0.0) + jnp.where(lane == i2, w2, 0.0)


def _router(x, g, w_pad):
    m = x.shape[0]
    tm = min(ROW_TILE, m)
    return pl.pallas_call(
        _router_body, grid=(m // tm,),
        in_specs=[pl.BlockSpec((tm, D_MODEL), lambda i: (i, 0)), pl.BlockSpec(g.shape, lambda i: (0, 0)),
                  pl.BlockSpec(w_pad.shape, lambda i: (0, 0))],
        out_specs=pl.BlockSpec((tm, LANES), lambda i: (i, 0)),
        out_shape=jax.ShapeDtypeStruct((m, LANES), f32),
        compiler_params=_cp(("parallel",)), name="router")(x, g, w_pad)


def _ffn_body(has_comb, has_final, x_ref, g_ref, *rest):
    rest = list(rest)
    comb_ref = rest.pop(0) if has_comb else None
    wg_ref, wu_ref, wd_ref = rest[:3]
    rest = rest[3:]
    gfin_ref = rest.pop(0) if has_final else None
    o_ref, hn_s, acc_s = rest
    e, j = pl.program_id(1), pl.program_id(2)
    first = (e == 0) & (j == 0)
    last = (e == pl.num_programs(1) - 1) & (j == pl.num_programs(2) - 1)

    @pl.when(first)
    def _():
        hn_s[...] = _rms(x_ref[...], g_ref[...]).astype(bf16)
        acc_s[...] = jnp.zeros_like(acc_s)

    hn = hn_s[...]
    h = _silu(_dot(hn, wg_ref[...].astype(bf16))) * _dot(hn, wu_ref[...].astype(bf16))
    if has_comb:
        comb = comb_ref[...]
        lane = lax.broadcasted_iota(jnp.int32, comb.shape, 1)
        h = h * jnp.sum(jnp.where(lane == e, comb, 0.0), axis=-1, keepdims=True)
    acc_s[...] += _dot(h.astype(bf16), wd_ref[...].astype(bf16))

    @pl.when(last)
    def _():
        y = x_ref[...] + acc_s[...]
        o_ref[...] = _rms(y, gfin_ref[...]) if has_final else y


def _ffn(x, g, comb, wg, wu, wd, gfin):
    m = x.shape[0]
    n_e, _, n_f = wg.shape
    tm = min(FFN_ROW_TILE, m)
    tf = FFN_COL_TILE
    in_specs = [pl.BlockSpec((tm, D_MODEL), lambda i, e, j: (i, 0)), pl.BlockSpec(g.shape, lambda i, e, j: (0, 0))]
    args = [x, g]
    if comb is not None:
        in_specs.append(pl.BlockSpec((tm, LANES), lambda i, e, j: (i, 0)))
        args.append(comb)
    in_specs += [pl.BlockSpec((None, D_MODEL, tf), lambda i, e, j: (e, 0, j)),
                 pl.BlockSpec((None, D_MODEL, tf), lambda i, e, j: (e, 0, j)),
                 pl.BlockSpec((None, tf, D_MODEL), lambda i, e, j: (e, j, 0))]
    args += [wg, wu, wd]
    if gfin is not None:
        in_specs.append(pl.BlockSpec(gfin.shape, lambda i, e, j: (0, 0)))
        args.append(gfin)
    return pl.pallas_call(
        functools.partial(_ffn_body, comb is not None, gfin is not None),
        grid=(m // tm, n_e, n_f // tf),
        in_specs=in_specs,
        out_specs=pl.BlockSpec((tm, D_MODEL), lambda i, e, j: (i, 0)),
        out_shape=jax.ShapeDtypeStruct(x.shape, f32),
        scratch_shapes=[pltpu.VMEM((tm, D_MODEL), bf16), pltpu.VMEM((tm, D_MODEL), f32)],
        compiler_params=_cp(("parallel", "arbitrary", "arbitrary")), name="ffn")(*args)


def _route_body(comb_ref, rank_ref, rankt_ref, cnt_ref):
    comb = comb_ref[...]
    tm = comb.shape[0]
    sel = comb > 0.0
    earlier = (lax.broadcasted_iota(jnp.int32, (tm, tm), 1) < lax.broadcasted_iota(jnp.int32, (tm, tm), 0))
    before = _dot(jnp.where(earlier, 1.0, 0.0).astype(bf16), jnp.where(sel, 1.0, 0.0).astype(bf16))
    rank = jnp.where(sel, before, -1.0)
    rank_ref[...] = rank
    rankt_ref[...] = rank.T
    cnt_ref[...] = jnp.sum(jnp.where(sel, 1.0, 0.0), axis=0, keepdims=True).astype(jnp.int32)


def _route(comb):
    m = comb.shape[0]
    tm = FFN_ROW_TILE
    return pl.pallas_call(
        _route_body, grid=(m // tm,),
        in_specs=[pl.BlockSpec((tm, LANES), lambda i: (i, 0))],
        out_specs=[pl.BlockSpec((tm, LANES), lambda i: (i, 0)), pl.BlockSpec((None, LANES, tm), lambda i: (i, 0, 0)),
                   pl.BlockSpec((None, 1, LANES), lambda i: (i, 0, 0))],
        out_shape=[jax.ShapeDtypeStruct((m, LANES), f32), jax.ShapeDtypeStruct((m // tm, LANES, tm), f32),
                   jax.ShapeDtypeStruct((m // tm, 1, LANES), jnp.int32)],
        compiler_params=_cp(("parallel",)), name="route")(comb)


def _moe_body(has_final, cnt_ref, x_ref, g_ref, comb_ref, rank_ref, rankt_ref, wg_ref, wu_ref, wd_ref, *rest):
    rest = list(rest)
    gfin_ref = rest.pop(0) if has_final else None
    o_ref, hn_s, acc_s, xe_s, ye_s = rest
    tm = x_ref.shape[0]
    i, e, j = pl.program_id(0), pl.program_id(1), pl.program_id(2)
    last_j = pl.num_programs(2) - 1
    n_tok = cnt_ref[i, e]

    @pl.when((e == 0) & (j == 0))
    def _():
        hn_s[...] = _rms(x_ref[...], g_ref[...]).astype(bf16)
        acc_s[...] = jnp.zeros_like(acc_s)

    wg, wu, wd = wg_ref[...].astype(bf16), wu_ref[...].astype(bf16), wd_ref[...].astype(bf16)
    lane = lax.broadcasted_iota(jnp.int32, (1, LANES), 1)
    for c in range(tm // MOE_CHUNK):
        rows = slice(c * MOE_CHUNK, (c + 1) * MOE_CHUNK)

        @pl.when(c * MOE_CHUNK < n_tok)
        def _(c=c, rows=rows):
            @pl.when(j == 0)
            def _():
                slot = (lax.broadcasted_iota(jnp.int32, (MOE_CHUNK, 1), 0) + c * MOE_CHUNK).astype(f32)
                pick = jnp.where(rankt_ref[pl.ds(e, 1), :] == slot, 1.0, 0.0).astype(bf16)
                xe_s[rows, :] = _dot(pick, hn_s[...]).astype(bf16)
                ye_s[rows, :] = jnp.zeros((MOE_CHUNK, D_MODEL), f32)

            xe = xe_s[rows, :]
            h = _silu(_dot(xe, wg)) * _dot(xe, wu)
            ye_s[rows, :] += _dot(h.astype(bf16), wd)

            @pl.when(j == last_j)
            def _():
                rank = jnp.sum(jnp.where(lane == e, rank_ref[...], 0.0), axis=-1, keepdims=True)
                gate = jnp.sum(jnp.where(lane == e, comb_ref[...], 0.0), axis=-1, keepdims=True)
                slot = (lax.broadcasted_iota(jnp.int32, (1, MOE_CHUNK), 1) + c * MOE_CHUNK).astype(f32)
                place = jnp.where(rank == slot, 1.0, 0.0).astype(bf16)
                y = ye_s[rows, :]
                y_hi = y.astype(bf16)
                y_lo = (y - y_hi.astype(f32)).astype(bf16)
                acc_s[...] += gate * (_dot(place, y_hi) + _dot(place, y_lo))

    @pl.when((e == pl.num_programs(1) - 1) & (j == last_j))
    def _():
        y = x_ref[...] + acc_s[...]
        o_ref[...] = _rms(y, gfin_ref[...]) if has_final else y


def _moe(x, g, comb, wg, wu, wd, gfin):
    m = x.shape[0]
    n_e, _, n_f = wg.shape
    tm, tf = FFN_ROW_TILE, FFN_COL_TILE
    rank, rank_t, cnt = _route(comb)
    in_specs = [pl.BlockSpec((tm, D_MODEL), lambda i, e, j, cnt: (i, 0)), pl.BlockSpec(g.shape, lambda i, e, j, cnt: (0, 0)),
                pl.BlockSpec((tm, LANES), lambda i, e, j, cnt: (i, 0)), pl.BlockSpec((tm, LANES), lambda i, e, j, cnt: (i, 0)),
                pl.BlockSpec((None, LANES, tm), lambda i, e, j, cnt: (i, 0, 0)),
                pl.BlockSpec((None, D_MODEL, tf), lambda i, e, j, cnt: (e, 0, j)),
                pl.BlockSpec((None, D_MODEL, tf), lambda i, e, j, cnt: (e, 0, j)),
                pl.BlockSpec((None, tf, D_MODEL), lambda i, e, j, cnt: (e, j, 0))]
    args = [x, g, comb, rank, rank_t, wg, wu, wd]
    if gfin is not None:
        in_specs.append(pl.BlockSpec(gfin.shape, lambda i, e, j, cnt: (0, 0)))
        args.append(gfin)
    return pl.pallas_call(
        functools.partial(_moe_body, gfin is not None),
        grid_spec=pltpu.PrefetchScalarGridSpec(
            num_scalar_prefetch=1, grid=(m // tm, n_e, n_f // tf), in_specs=in_specs,
            out_specs=pl.BlockSpec((tm, D_MODEL), lambda i, e, j, cnt: (i, 0)),
            scratch_shapes=[pltpu.VMEM((tm, D_MODEL), bf16), pltpu.VMEM((tm, D_MODEL), f32),
                            pltpu.VMEM((tm, D_MODEL), bf16), pltpu.VMEM((tm, D_MODEL), f32)]),
        out_shape=jax.ShapeDtypeStruct(x.shape, f32),
        compiler_params=_cp(("parallel", "arbitrary", "arbitrary")), name="moe")(cnt.reshape(m // tm, LANES), *args)


def _page_spec(layer, slot, chunk=lambda c: c):
    return pl.BlockSpec((None, None, N_HEADS, HEAD_DIM, PAGE_SIZE),
                        lambda b, c, pt: (layer, pt[b, chunk(c) * DEC_PAGES + slot], 0, 0, 0))


def _logf_gather_body(pt_ref, *refs):
    o_ref = refs[-1]
    b, c = pl.program_id(0), pl.program_id(1)
    for s in range(DEC_PAGES):
        row = pt_ref[b, c * DEC_PAGES + s] % SUBLANES
        o_ref[:, pl.ds(s, 1), :] = refs[s][:, pl.ds(row, 1), :]


def _logf_gather(page_table, logf_t, layer):
    b, n_pages = page_table.shape

    def spec(slot):
        return pl.BlockSpec((None, N_HEADS, SUBLANES, PAGE_SIZE),
                            lambda i, c, pt: (layer, 0, pt[i, c * DEC_PAGES + slot] // SUBLANES, 0))

    return pl.pallas_call(
        _logf_gather_body,
        grid_spec=pltpu.PrefetchScalarGridSpec(
            num_scalar_prefetch=1, grid=(b, n_pages // DEC_PAGES),
            in_specs=[spec(s) for s in range(DEC_PAGES)],
            out_specs=pl.BlockSpec((None, N_HEADS, DEC_PAGES, PAGE_SIZE), lambda i, c, pt: (i, 0, c, 0))),
        out_shape=jax.ShapeDtypeStruct((b, N_HEADS, n_pages, PAGE_SIZE), f32),
        compiler_params=_cp(("parallel", "arbitrary")), name="logf_gather")(page_table, *([logf_t] * DEC_PAGES))


def _suffix_body(lf_ref, new_ref, o_ref):
    x = lf_ref[...]
    n = x.shape[1]
    pos = lax.broadcasted_iota(jnp.int32, x.shape, 1)
    y = x
    s = 1
    while s < n:
        y = y + jnp.where(pos + s < n, pltpu.roll(y, n - s, axis=1), 0.0)
        s *= 2
    o_ref[...] = (y - x) + new_ref[:, 0:1]


def _suffix_bias(lf_t, lf_new):
    b, _, n = lf_t.shape
    return pl.pallas_call(
        _suffix_body, grid=(b,),
        in_specs=[pl.BlockSpec((None, SUBLANES, n), lambda i: (i, 0, 0)),
                  pl.BlockSpec((None, SUBLANES, LANES), lambda i: (i, 0, 0))],
        out_specs=pl.BlockSpec((None, SUBLANES, n), lambda i: (i, 0, 0)),
        out_shape=jax.ShapeDtypeStruct(lf_t.shape, f32),
        compiler_params=_cp(("parallel",)), name="suffix_bias")(lf_t, lf_new)


def _block_diag(x):
    b = x.shape[0]
    eye = jnp.eye(SUBLANES, N_HEADS, dtype=x.dtype)
    return (eye[None, :, :, None] * x[:, None, :, :]).reshape(b, SUBLANES, D_ATT)


def _rows_of_head(x):
    eye = jnp.eye(N_HEADS, SUBLANES, dtype=x.dtype)
    return eye[None, :, :, None] * x[:, :, None, :]


def _flat_page(ref):
    return ref[...].reshape(D_ATT, PAGE_SIZE).astype(bf16)


def _fox_dec_body(pt_ref, q_ref, kn_ref, vn_ref, bias_ref, *refs):
    k_refs, v_refs = refs[:DEC_PAGES], refs[DEC_PAGES:2 * DEC_PAGES]
    o_ref, s_all, m_s, inv_s, acc_s = refs[2 * DEC_PAGES:]
    nc = s_all.shape[0]
    c = pl.program_id(1)
    q = q_ref[...]

    @pl.when(c == 0)
    def _():
        m_s[...] = jnp.full_like(m_s, -jnp.inf)

    @pl.when(c < nc)
    def _():
        qb = q.astype(bf16)
        s = jnp.concatenate([_dot(qb, _flat_page(k_refs[g])) for g in range(DEC_PAGES)], axis=1) + bias_ref[...]
        s_all[c] = s
        m_s[...] = jnp.maximum(m_s[...], jnp.max(s, axis=-1, keepdims=True))

    @pl.when(c == nc)
    def _():
        s_new = jnp.sum(_round_bf16(q) * _round_bf16(kn_ref[...]), axis=-1, keepdims=True)
        m = jnp.maximum(m_s[...], s_new)
        p_new = jnp.exp(s_new - m)
        l = p_new
        for n in range(nc):
            p = jnp.exp(s_all[n] - m)
            s_all[n] = p
            l = l + jnp.sum(p, axis=-1, keepdims=True)
        inv = 1.0 / l
        inv_s[...] = inv
        acc_s[...] = _round_bf16(p_new * inv) * _round_bf16(vn_ref[...])

    @pl.when(c >= nc)
    def _():
        p = (s_all[c - nc] * inv_s[...]).astype(bf16)
        pv = acc_s[...]
        for g in range(DEC_PAGES):
            pv = pv + _dot_nt(p[:, g * PAGE_SIZE:(g + 1) * PAGE_SIZE], _flat_page(v_refs[g]))
        acc_s[...] = pv

    @pl.when(c == 2 * nc - 1)
    def _():
        row = lax.broadcasted_iota(jnp.int32, acc_s.shape, 0)
        col = lax.broadcasted_iota(jnp.int32, acc_s.shape, 1)
        own = (col >= row * HEAD_DIM) & (col < (row + 1) * HEAD_DIM)
        o_ref[...] = jnp.sum(jnp.where(own, acc_s[...], 0.0), axis=0, keepdims=True)


def _fox_decode(page_table, q, k_new, v_new, bias, cache_k, cache_v, layer):
    b, n_pages = page_table.shape
    nc = n_pages // DEC_PAGES
    small = pl.BlockSpec((None, SUBLANES, D_ATT), lambda i, c, pt: (i, 0, 0))
    k_chunk = lambda c: jnp.minimum(c, nc - 1)
    v_chunk = lambda c: jnp.maximum(c - nc, 0)
    return pl.pallas_call(
        _fox_dec_body,
        grid_spec=pltpu.PrefetchScalarGridSpec(
            num_scalar_prefetch=1, grid=(b, 2 * nc),
            in_specs=[small, small, small,
                      pl.BlockSpec((None, SUBLANES, DEC_PAGES * PAGE_SIZE), lambda i, c, pt: (i, 0, k_chunk(c)))]
            + [_page_spec(layer, s, k_chunk) for s in range(DEC_PAGES)]
            + [_page_spec(layer, s, v_chunk) for s in range(DEC_PAGES)],
            out_specs=pl.BlockSpec((None, 1, D_ATT), lambda i, c, pt: (i, 0, 0)),
            scratch_shapes=[pltpu.VMEM((nc, SUBLANES, DEC_PAGES * PAGE_SIZE), f32), pltpu.VMEM((SUBLANES, 1), f32),
                            pltpu.VMEM((SUBLANES, 1), f32), pltpu.VMEM((SUBLANES, D_ATT), f32)]),
        out_shape=jax.ShapeDtypeStruct((b, 1, D_ATT), f32),
        compiler_params=_cp(("parallel", "arbitrary")), name="fox_decode")(
            page_table, q, k_new, v_new, bias, *([cache_k] * DEC_PAGES), *([cache_v] * DEC_PAGES))


def _gate_body(pt_ref, q_ref, *refs):
    o_ref = refs[-1]
    pages_per_block = MOBA_BLOCK // PAGE_SIZE
    q = _round_bf16(q_ref[...])
    for blk in range(DEC_PAGES // pages_per_block):
        ksum = refs[blk * pages_per_block][...]
        for s in range(1, pages_per_block):
            ksum = ksum + refs[blk * pages_per_block + s][...]
        kmean = jnp.sum(ksum, axis=-1, keepdims=True) * (1.0 / MOBA_BLOCK)
        o_ref[blk] = jnp.sum(_round_bf16(kmean) * q, axis=1)


def _gate_decode(page_table, q_col, cache_k, layer):
    b, n_pages = page_table.shape
    per_step = DEC_PAGES * PAGE_SIZE // MOBA_BLOCK
    return pl.pallas_call(
        _gate_body,
        grid_spec=pltpu.PrefetchScalarGridSpec(
            num_scalar_prefetch=1, grid=(b, n_pages // DEC_PAGES),
            in_specs=[pl.BlockSpec((None, N_HEADS, HEAD_DIM, 1), lambda i, c, pt: (i, 0, 0, 0))]
            + [_page_spec(layer, s) for s in range(DEC_PAGES)],
            out_specs=pl.BlockSpec((None, per_step, N_HEADS, 1), lambda i, c, pt: (i, c, 0, 0))),
        out_shape=jax.ShapeDtypeStruct((b, n_pages * PAGE_SIZE // MOBA_BLOCK, N_HEADS, 1), f32),
        compiler_params=_cp(("parallel", "arbitrary")), name="gate_decode")(page_table, q_col, *([cache_k] * DEC_PAGES))


def _select_body(g_ref, o_ref):
    g = g_ref[...]
    nb = g.shape[0]
    blk = lax.broadcasted_iota(jnp.int32, g.shape, 0)
    for t in range(MOBA_TOPK):
        mx = jnp.max(g, axis=0, keepdims=True)
        idx = jnp.min(jnp.where(g == mx, blk, nb), axis=0, keepdims=True)
        o_ref[t] = idx[0]
        g = jnp.where(blk == idx, -jnp.inf, g)


def _select_decode(gates):
    b, nb = gates.shape[:2]
    return pl.pallas_call(
        _select_body, grid=(b,),
        in_specs=[pl.BlockSpec((None, nb, N_HEADS, 1), lambda i: (i, 0, 0, 0))],
        out_specs=pl.BlockSpec((None, MOBA_TOPK, N_HEADS, 1), lambda i: (i, 0, 0, 0)),
        out_shape=jax.ShapeDtypeStruct((b, MOBA_TOPK, N_HEADS, 1), jnp.int32),
        compiler_params=_cp(("parallel",)), name="select_decode")(gates)


def _moba_dec_body(past_len, pt_ref, idx_ref, q_ref, kn_ref, vn_ref, tab_ref, *refs):
    pages_per_block = MOBA_BLOCK // PAGE_SIZE
    n_sel = MOBA_TOPK * pages_per_block
    k_refs, v_refs, o_ref = refs[:n_sel], refs[n_sel:2 * n_sel], refs[2 * n_sel]
    b, h = pl.program_id(0), pl.program_id(1)
    q = q_ref[...]
    qb = q.astype(bf16)
    tab = tab_ref[...]
    lane = lax.broadcasted_iota(jnp.int32, (1, PAGE_SIZE), 1)

    def t5(dist):
        bucket = _t5_bucket(dist)
        out = jnp.zeros((SUBLANES, dist.shape[1]), f32)
        for nbk in range(N_BUCKETS):
            out = jnp.where(bucket == nbk, tab[:, nbk:nbk + 1], out)
        return out

    scores = []
    for t in range(MOBA_TOPK):
        blk = idx_ref[b, t, h]
        for s in range(pages_per_block):
            kpos = blk * MOBA_BLOCK + s * PAGE_SIZE + lane
            scores.append(_dot(qb, k_refs[t * pages_per_block + s][...].astype(bf16)) + t5(past_len - kpos))
    s_past = jnp.concatenate(scores, axis=1)
    s_new = (jnp.sum(_round_bf16(q) * _round_bf16(kn_ref[...]), axis=-1, keepdims=True)
             + t5(jnp.zeros((1, 1), jnp.int32)))
    m = jnp.maximum(jnp.max(s_past, axis=-1, keepdims=True), s_new)
    p = jnp.exp(s_past - m)
    p_new = jnp.exp(s_new - m)
    inv = 1.0 / (jnp.sum(p, axis=-1, keepdims=True) + p_new)
    p = (p * inv).astype(bf16)
    acc = _round_bf16(p_new * inv) * _round_bf16(vn_ref[...])
    for g in range(n_sel):
        acc = acc + _dot_nt(p[:, g * PAGE_SIZE:(g + 1) * PAGE_SIZE], v_refs[g][...].astype(bf16))
    row = lax.broadcasted_iota(jnp.int32, (SUBLANES, HEAD_DIM), 0)
    res = jnp.where(row == h, acc, 0.0)

    @pl.when(h == 0)
    def _():
        o_ref[...] = res

    @pl.when(h > 0)
    def _():
        o_ref[...] += res


def _moba_decode(page_table, idx, q, k_new, v_new, tab_t, cache_k, cache_v, layer):
    b, n_pages = page_table.shape
    pages_per_block = MOBA_BLOCK // PAGE_SIZE
    small = pl.BlockSpec((None, None, SUBLANES, HEAD_DIM), lambda i, h, pt, ix: (i, h, 0, 0))

    def sel_spec(t, s):
        return pl.BlockSpec((None, None, None, HEAD_DIM, PAGE_SIZE),
                            lambda i, h, pt, ix: (layer, pt[i, ix[i, t, h] * pages_per_block + s], h, 0, 0))

    specs = [sel_spec(t, s) for t in range(MOBA_TOPK) for s in range(pages_per_block)]
    return pl.pallas_call(
        functools.partial(_moba_dec_body, n_pages * PAGE_SIZE),
        grid_spec=pltpu.PrefetchScalarGridSpec(
            num_scalar_prefetch=2, grid=(b, N_HEADS),
            in_specs=[small, small, small, pl.BlockSpec(tab_t.shape, lambda i, h, pt, ix: (0, 0))] + specs * 2,
            out_specs=pl.BlockSpec((None, SUBLANES, HEAD_DIM), lambda i, h, pt, ix: (i, 0, 0))),
        out_shape=jax.ShapeDtypeStruct((b, SUBLANES, HEAD_DIM), f32),
        compiler_params=_cp(("parallel", "arbitrary")), name="moba_decode")(
            page_table, idx, q, k_new, v_new, tab_t, *([cache_k] * len(specs)), *([cache_v] * len(specs)))


def _split_w_in(w):
    o_f = 2 * D_CONV
    o_g = o_f + 3 * D_ATT
    o_m = o_g + N_HEADS
    wg = jnp.zeros((D_MODEL, LANES), f32).at[:, :N_HEADS].set(w[:, o_g:o_m])
    return w[:, :o_f].astype(bf16), w[:, o_f:o_g].astype(bf16), w[:, o_m:].astype(bf16), wg.astype(bf16)


def _pad_lanes(v):
    return jnp.zeros((1, LANES), f32).at[0, :v.shape[0]].set(v)


def _layer_weights(l, p):
    wglu, wf, wm, wg = _split_w_in(p["w_in"][l])
    i = l // 2
    d = dict(norm_mix=p["norm_mix"][l][None], norm_ffn=p["norm_ffn"][l][None],
             wglu=wglu, wf=wf, wm=wm, wg=wg, bfg=_pad_lanes(p["b_forget"][l]),
             conv_w=p["conv_w"][l], conv_b=p["conv_b"][l][None], ln_g=p["conv_ln_g"][l][None], ln_b=p["conv_ln_b"][l][None],
             w_pw=p["w_conv_pw"][l].astype(bf16), w_out=p["w_out"][l].astype(bf16))
    if l % 2 == 0:
        d.update(ffn_g=p["w_dense_gate"][i][None], ffn_u=p["w_dense_up"][i][None], ffn_d=p["w_dense_down"][i][None], router=None)
    else:
        d.update(ffn_g=p["w_moe_gate"][i], ffn_u=p["w_moe_up"][i], ffn_d=p["w_moe_down"][i],
                 router=jnp.zeros((D_MODEL, LANES), f32).at[:, :N_EXPERTS].set(p["w_router"][i]).astype(bf16))
    return d


def _channel_mix(x, w, gfin):
    comb = None if w["router"] is None else _router(x, w["norm_ffn"], w["router"])
    if comb is not None and x.shape[0] % FFN_ROW_TILE == 0:
        return _moe(x, w["norm_ffn"], comb, w["ffn_g"], w["ffn_u"], w["ffn_d"], gfin)
    return _ffn(x, w["norm_ffn"], comb, w["ffn_g"], w["ffn_u"], w["ffn_d"], gfin)


def _heads(a, b, t):
    return a.reshape(b, t, N_HEADS, HEAD_DIM)


def _prompt_layer(x, w, rel_bias, t5, b, t, gfin):
    u, ff, fm, bff, bfm, lf, lft, fvt, mvt = _proj(x, w["norm_mix"], w["wglu"], w["wf"], w["wm"], w["wg"], w["bfg"], True)
    u3 = u.reshape(b, t, D_CONV)
    up = jnp.concatenate([jnp.zeros((b, CONV_HALO, D_CONV), f32), u3], axis=1)
    conv_out = _conv_prompt(up, w["conv_w"], w["conv_b"], w["ln_g"], w["ln_b"], w["w_pw"], t).reshape(b * t, D_CONV)
    ccol, crow = _cumsum(lf, lft, b, t)
    nq = t // min(ATT_TILE, t)
    crow3 = crow.reshape(SUBLANES, b * nq, t // nq).transpose(1, 0, 2)
    fox_out = _fox_prompt(bff, fvt, ccol, crow3, b, t)
    kmean = _kmean_prompt(fm, b, t)
    moba_out = _moba_prompt(rel_bias, bfm, mvt, kmean, t5, b, t)
    x = _outproj(x, conv_out, fox_out, moba_out, w["w_out"])
    x = _channel_mix(x, w, gfin)
    state = (_heads(ff[:, D_ATT:2 * D_ATT], b, t), _heads(ff[:, 2 * D_ATT:], b, t), lf[:, :N_HEADS].reshape(b, t, N_HEADS),
             _heads(fm[:, D_ATT:2 * D_ATT], b, t), _heads(fm[:, 2 * D_ATT:], b, t), u3[:, t - (CONV_WIDTH - 1):])
    return x, state


def _sample_layer(x, w, l, rel_bias, caches, page_table, gfin):
    cache_fk, cache_fv, logf_t, cache_mk, cache_mv, state_t = caches
    b = x.shape[0]
    n_past = page_table.shape[1] * PAGE_SIZE
    u, ff, fm, _, _, lf = _proj(x, w["norm_mix"], w["wglu"], w["wf"], w["wm"], w["wg"], w["bfg"], False)
    conv_out = _conv_step(state_t[l], u, w["conv_w"], w["conv_b"], w["ln_g"], w["ln_b"], w["w_pw"])
    heads = lambda a: a.reshape(b, N_HEADS, HEAD_DIM)
    fq, fk, fv = heads(ff[:, :D_ATT]) * Q_SCALE, heads(ff[:, D_ATT:2 * D_ATT]), heads(ff[:, 2 * D_ATT:])
    mq, mk, mv = heads(fm[:, :D_ATT]) * Q_SCALE, heads(fm[:, D_ATT:2 * D_ATT]), heads(fm[:, 2 * D_ATT:])
    lf_pages = _logf_gather(page_table, logf_t, l)
    lf_t = jnp.pad(lf_pages.reshape(b, N_HEADS, n_past), ((0, 0), (0, SUBLANES - N_HEADS), (0, 0)))
    lf_new = jnp.broadcast_to(jnp.pad(lf[:, :N_HEADS], ((0, 0), (0, SUBLANES - N_HEADS)))[:, :, None], (b, SUBLANES, LANES))
    bias = _suffix_bias(lf_t, lf_new)
    fox_out = _fox_decode(page_table, _block_diag(fq), _block_diag(fk), _block_diag(fv), bias,
                          cache_fk, cache_fv, l).reshape(b, D_ATT)
    idx = _select_decode(_gate_decode(page_table, mq[..., None], cache_mk, l))[..., 0]
    tab_t = jnp.pad(rel_bias.T, ((0, SUBLANES - N_HEADS), (0, 0)))
    moba_out = _moba_decode(page_table, idx, _rows_of_head(mq), _rows_of_head(mk), _rows_of_head(mv), tab_t,
                            cache_mk, cache_mv, l)[:, :N_HEADS].reshape(b, D_ATT)
    x = _outproj(x, conv_out, fox_out, moba_out, w["w_out"])
    x = _channel_mix(x, w, gfin)
    conv_state_t = jnp.concatenate([state_t[l][1:], u[None]], axis=0)
    state = (fk[:, None], fv[:, None], lf[:, None, :N_HEADS], mk[:, None], mv[:, None], conv_state_t)
    return x, state


def kernel(x_prompt, x_sample, cache_fox_k, cache_fox_v, cache_fox_logf, cache_moba_k, cache_moba_v, state_conv, page_table, norm_mix, norm_ffn, norm_final, w_in, b_forget, conv_w, conv_b, conv_ln_g, conv_ln_b, w_conv_pw, rel_bias, w_out, w_dense_gate, w_dense_up, w_dense_down, w_router, w_moe_gate, w_moe_up, w_moe_down):
    p = dict(norm_mix=norm_mix, norm_ffn=norm_ffn, w_in=w_in, b_forget=b_forget, conv_w=conv_w, conv_b=conv_b,
             conv_ln_g=conv_ln_g, conv_ln_b=conv_ln_b, w_conv_pw=w_conv_pw, w_out=w_out, w_dense_gate=w_dense_gate,
             w_dense_up=w_dense_up, w_dense_down=w_dense_down, w_router=w_router, w_moe_gate=w_moe_gate,
             w_moe_up=w_moe_up, w_moe_down=w_moe_down)
    depth = norm_mix.shape[0]
    weights = [_layer_weights(l, p) for l in range(depth)]
    gfin = norm_final[None]
    t5 = _t5_tiles(rel_bias)

    b, t, _ = x_prompt.shape
    x = x_prompt.reshape(b * t, D_MODEL)
    states_p = []
    for l in range(depth):
        x, st = _prompt_layer(x, weights[l], rel_bias, t5, b, t, gfin if l == depth - 1 else None)
        states_p.append(st)
    y_prompt = x.reshape(b, t, D_MODEL)

    bs = x_sample.shape[0]
    rows_on_lanes = lambda c: jnp.transpose(c, (0, 1, 3, 4, 2))
    caches = (rows_on_lanes(cache_fox_k), rows_on_lanes(cache_fox_v), jnp.transpose(cache_fox_logf, (0, 3, 1, 2)),
              rows_on_lanes(cache_moba_k), rows_on_lanes(cache_moba_v), jnp.transpose(state_conv, (0, 2, 1, 3)))
    x = x_sample.reshape(bs, D_MODEL)
    states_s = []
    for l in range(depth):
        x, st = _sample_layer(x, weights[l], l, rel_bias, caches, page_table, gfin if l == depth - 1 else None)
        states_s.append(st)
    y_sample = x.reshape(bs, 1, D_MODEL)

    stack = lambda sts: [jnp.stack(s) for s in zip(*sts)]
    out_s = stack(states_s)
    out_s[-1] = jnp.transpose(out_s[-1], (0, 2, 1, 3))
    return (y_prompt, y_sample, *stack(states_p), *out_s)
```

```python
import functools
import math

import jax
import jax.numpy as jnp
from jax import lax
from jax.experimental import pallas as pl
from jax.experimental.pallas import tpu as pltpu

f32 = jnp.float32
bf16 = jnp.bfloat16

D_MODEL = 1024
HEAD_DIM = 64
N_HEADS = 6
D_ATT = N_HEADS * HEAD_DIM
D_CONV = 256
CONV_WIDTH = 31
PAGE_SIZE = 128
MOBA_BLOCK = 256
MOBA_TOPK = 3
N_BUCKETS = 32
MAX_DISTANCE = 128
N_EXPERTS = 8
Q_SCALE = HEAD_DIM ** -0.5

LANES = 128
SUBLANES = 8
VMEM_LIMIT = 56 * 1024 * 1024

ROW_TILE = 256
ATT_TILE = 256
ATT_HEAD_GROUP = 6
CONV_TILE = 512
CONV_HALO = 32
FFN_ROW_TILE = 1024
FFN_COL_TILE = 256
MOE_COL_TILE = 512
MOE_CHUNK = 256
DEC_PAGES = 16


def _cp(sem):
    return pltpu.CompilerParams(dimension_semantics=sem, vmem_limit_bytes=VMEM_LIMIT)


def _rms(x, g):
    return (x * lax.rsqrt(jnp.mean(x * x, axis=-1, keepdims=True) + 1e-6)) * g


def _log_sigmoid(x):
    return jnp.minimum(x, 0.0) - jnp.log(1.0 + jnp.exp(-jnp.abs(x)))


def _silu(x):
    return x * jax.nn.sigmoid(x)


def _round_bf16(x):
    return x.astype(bf16).astype(f32)


def _dot(a, b):
    return jnp.dot(a, b, preferred_element_type=f32)


def _dot_nt(a, b, precision=None):
    return lax.dot_general(a, b, (((1,), (1,)), ((), ())), preferred_element_type=f32, precision=precision)


def _t5_bucket(d):
    max_exact = N_BUCKETS // 2
    df = jnp.maximum(d, max_exact).astype(f32)
    large = max_exact + (jnp.log(df / max_exact) / math.log(MAX_DISTANCE / max_exact)
                         * (N_BUCKETS - max_exact)).astype(jnp.int32)
    return jnp.where(d < max_exact, d, jnp.minimum(large, N_BUCKETS - 1))


def _proj_body(with_t, x_ref, g_ref, wglu_ref, wf_ref, wm_ref, wg_ref, bfg_ref,
               u_ref, ff_ref, fm_ref, bff_ref, bfm_ref, lf_ref, *maybe_lft):
    hb = _rms(x_ref[...], g_ref[...]).astype(bf16)
    glu = _dot(hb, wglu_ref[...])
    u_ref[...] = glu[:, :D_CONV] * jax.nn.sigmoid(glu[:, D_CONV:])
    col = lax.broadcasted_iota(jnp.int32, (1, 3 * D_ATT), 1)
    qscale = jnp.where(col < D_ATT, Q_SCALE, 1.0).astype(f32)
    pf = _dot(hb, wf_ref[...])
    ff_ref[...] = pf
    bff_ref[...] = (pf * qscale).astype(bf16)
    pm = _dot(hb, wm_ref[...])
    fm_ref[...] = pm
    bfm_ref[...] = (pm * qscale).astype(bf16)
    lf = _log_sigmoid(_dot(hb, wg_ref[...]) + bfg_ref[...])
    lf_ref[...] = lf
    if with_t:
        lft_ref, fvt_ref, mvt_ref = maybe_lft
        lft_ref[...] = lf.T[:SUBLANES]
        fvt_ref[...] = pf[:, 2 * D_ATT:].T.astype(bf16)
        mvt_ref[...] = pm[:, 2 * D_ATT:].T.astype(bf16)


def _proj(x, g, wglu, wf, wm, wg, bfg, with_t):
    m = x.shape[0]
    tm = min(ROW_TILE, m)
    row = lambda w: pl.BlockSpec((tm, w), lambda i: (i, 0))
    full = lambda a: pl.BlockSpec(a.shape, lambda i: (0, 0))
    out_shape = [jax.ShapeDtypeStruct((m, D_CONV), f32),
                 jax.ShapeDtypeStruct((m, 3 * D_ATT), f32), jax.ShapeDtypeStruct((m, 3 * D_ATT), f32),
                 jax.ShapeDtypeStruct((m, 3 * D_ATT), bf16), jax.ShapeDtypeStruct((m, 3 * D_ATT), bf16),
                 jax.ShapeDtypeStruct((m, LANES), f32)]
    out_specs = [row(D_CONV), row(3 * D_ATT), row(3 * D_ATT), row(3 * D_ATT), row(3 * D_ATT), row(LANES)]
    if with_t:
        out_shape.append(jax.ShapeDtypeStruct((SUBLANES, m), f32))
        out_specs.append(pl.BlockSpec((SUBLANES, tm), lambda i: (0, i)))
        for _ in range(2):
            out_shape.append(jax.ShapeDtypeStruct((m // tm, D_ATT, tm), bf16))
            out_specs.append(pl.BlockSpec((None, D_ATT, tm), lambda i: (i, 0, 0)))
    return pl.pallas_call(
        functools.partial(_proj_body, with_t),
        grid=(m // tm,),
        in_specs=[row(D_MODEL), full(g), full(wglu), full(wf), full(wm), full(wg), full(bfg)],
        out_specs=out_specs, out_shape=out_shape,
        compiler_params=_cp(("parallel",)), name="proj")(x, g, wglu, wf, wm, wg, bfg)


def _cumsum_body(lf_ref, lft_ref, ccol_ref, crow_ref):
    x = lf_ref[...]
    t = x.shape[0]
    pos = lax.broadcasted_iota(jnp.int32, x.shape, 0)
    s = 1
    while s < t:
        x = x + jnp.where(pos >= s, pltpu.roll(x, s, axis=0), 0.0)
        s *= 2
    ccol_ref[...] = x
    y = lft_ref[...]
    pos = lax.broadcasted_iota(jnp.int32, y.shape, 1)
    s = 1
    while s < t:
        y = y + jnp.where(pos >= s, pltpu.roll(y, s, axis=1), 0.0)
        s *= 2
    crow_ref[...] = y


def _cumsum(lf, lft, b, t):
    return pl.pallas_call(
        _cumsum_body, grid=(b,),
        in_specs=[pl.BlockSpec((t, LANES), lambda i: (i, 0)), pl.BlockSpec((SUBLANES, t), lambda i: (0, i))],
        out_specs=[pl.BlockSpec((t, LANES), lambda i: (i, 0)), pl.BlockSpec((SUBLANES, t), lambda i: (0, i))],
        out_shape=[jax.ShapeDtypeStruct(lf.shape, f32), jax.ShapeDtypeStruct(lft.shape, f32)],
        compiler_params=_cp(("parallel",)), name="cumsum")(lf, lft)


def _conv_tail(y, lng_ref, lnb_ref, wpw_ref):
    mu = jnp.mean(y, axis=-1, keepdims=True)
    var = jnp.mean(jnp.square(y - mu), axis=-1, keepdims=True)
    z = _silu((y - mu) * lax.rsqrt(var + 1e-5) * lng_ref[...] + lnb_ref[...])
    return _dot(z.astype(bf16), wpw_ref[...])


def _conv_body(main_ref, halo_ref, w_ref, b_ref, lng_ref, lnb_ref, wpw_ref, o_ref):
    tt = main_ref.shape[0]
    win = _round_bf16(jnp.concatenate([main_ref[...], halo_ref[...]], axis=0))
    w = _round_bf16(w_ref[...])
    lead = CONV_HALO - (CONV_WIDTH - 1)
    acc = jnp.zeros((tt, D_CONV), f32) + b_ref[...]
    for k in range(CONV_WIDTH):
        acc = acc + win[lead + k:lead + k + tt] * w[k:k + 1]
    o_ref[...] = _conv_tail(acc, lng_ref, lnb_ref, wpw_ref)


def _conv_prompt(up, w, b, lng, lnb, wpw, t):
    bsz = up.shape[0]
    tt = min(CONV_TILE, t)
    full = lambda a: pl.BlockSpec(a.shape, lambda i, j: (0, 0))
    return pl.pallas_call(
        _conv_body, grid=(bsz, t // tt),
        in_specs=[pl.BlockSpec((None, tt, D_CONV), lambda i, j: (i, j, 0)),
                  pl.BlockSpec((None, CONV_HALO, D_CONV), lambda i, j: (i, (j + 1) * (tt // CONV_HALO), 0)),
                  full(w), full(b), full(lng), full(lnb), full(wpw)],
        out_specs=pl.BlockSpec((None, tt, D_CONV), lambda i, j: (i, j, 0)),
        out_shape=jax.ShapeDtypeStruct((bsz, t, D_CONV), f32),
        compiler_params=_cp(("parallel", "parallel")), name="conv_prompt")(up, up, w, b, lng, lnb, wpw)


def _conv_step_body(st_ref, u_ref, w_ref, b_ref, lng_ref, lnb_ref, wpw_ref, o_ref):
    w = w_ref[...]
    acc = u_ref[...] * w[CONV_WIDTH - 1:CONV_WIDTH] + b_ref[...]
    for k in range(CONV_WIDTH - 1):
        acc = acc + st_ref[k] * w[k:k + 1]
    o_ref[...] = _conv_tail(acc, lng_ref, lnb_ref, wpw_ref)


def _conv_step(state_t, u, w, b, lng, lnb, wpw):
    return pl.pallas_call(
        _conv_step_body, out_shape=jax.ShapeDtypeStruct(u.shape, f32), name="conv_step")(state_t, u, w, b, lng, lnb, wpw)


def _flash_step(qh, k, vt, bias_row, bias_col, bias_tile, mask, carry):
    m, l, acc = carry
    s = _dot_nt(k, qh)
    if bias_tile is not None:
        s = s + bias_tile
    if bias_col is not None:
        s = s - bias_col
    if bias_row is not None:
        s = s + bias_row
    if mask is not None:
        s = jnp.where(mask, s, -jnp.inf)
    m_new = jnp.maximum(m, jnp.max(s, axis=0, keepdims=True))
    alpha = jnp.exp(m - m_new)
    p = jnp.exp(s - m_new)
    l = alpha * l + jnp.sum(p, axis=0, keepdims=True)
    acc = alpha * acc + _dot(vt, p.astype(bf16))
    return m_new, l, acc


def _flash_init(tq):
    return (jnp.full((1, tq), -jnp.inf, f32), jnp.zeros((1, tq), f32), jnp.zeros((HEAD_DIM, tq), f32))


def _masked_queries(q_ref, h):
    j, hh = divmod(h, 2)
    lane = lax.broadcasted_iota(jnp.int32, (1, LANES), 1)
    qp = q_ref[:, j * LANES:(j + 1) * LANES]
    return jnp.where((lane >= hh * HEAD_DIM) & (lane < (hh + 1) * HEAD_DIM), qp, jnp.zeros_like(qp))


def _pair_cols(h):
    return slice((h // 2) * LANES, (h // 2 + 1) * LANES)


def _head_rows(h):
    return slice(h * HEAD_DIM, (h + 1) * HEAD_DIM)


def _fox_body(q_ref, k_ref, vt_ref, ccol_ref, crow_ref, o_ref):
    tq = q_ref.shape[0]
    i = pl.program_id(1)
    key = lax.broadcasted_iota(jnp.int32, (tq, tq), 0)
    qry = lax.broadcasted_iota(jnp.int32, (tq, tq), 1)
    causal = key <= qry
    outs = []
    for g0 in range(0, N_HEADS, ATT_HEAD_GROUP):
        group = range(g0, g0 + ATT_HEAD_GROUP)
        qhs = [_masked_queries(q_ref, h) for h in group]
        cqs = [crow_ref[i, h:h + 1, :] for h in group]

        def tile(n, carries, mask, group=group, qhs=qhs, cqs=cqs):
            rows = pl.ds(pl.multiple_of(n * tq, tq), tq)
            new = []
            for h, qh, cq, carry in zip(group, qhs, cqs, carries):
                ck = ccol_ref[rows, h:h + 1]
                new.append(_flash_step(qh, k_ref[rows, _pair_cols(h)], vt_ref[n, _head_rows(h), :], cq, ck, None, mask, carry))
            return tuple(new)

        init = tuple(_flash_init(tq) for _ in group)
        carries = lax.fori_loop(0, i, lambda n, cr, tile=tile: tile(n, cr, None), init)
        outs += [acc / l for _, l, acc in tile(i, carries, causal)]
    o_ref[...] = jnp.concatenate(outs, axis=0).T


def _fox_prompt(qkv, vt, ccol, crow3, b, t):
    tq = min(ATT_TILE, t)
    nq = t // tq
    return pl.pallas_call(
        _fox_body, grid=(b, nq),
        in_specs=[pl.BlockSpec((tq, D_ATT), lambda i, j: (i * nq + j, 0)),
                  pl.BlockSpec((t, D_ATT), lambda i, j: (i, 1)),
                  pl.BlockSpec((nq, D_ATT, tq), lambda i, j: (i, 0, 0)),
                  pl.BlockSpec((t, LANES), lambda i, j: (i, 0)),
                  pl.BlockSpec((nq, SUBLANES, tq), lambda i, j: (i, 0, 0))],
        out_specs=pl.BlockSpec((tq, D_ATT), lambda i, j: (i * nq + j, 0)),
        out_shape=jax.ShapeDtypeStruct((b * t, D_ATT), f32),
        compiler_params=_cp(("parallel", "arbitrary")), name="fox_prompt")(qkv, qkv, vt, ccol, crow3)


def _kmean_body(k_ref, o_ref):
    t = k_ref.shape[0]
    k = k_ref[...].reshape(t // MOBA_BLOCK, MOBA_BLOCK, D_ATT)
    o_ref[...] = jnp.mean(k, axis=1)


def _kmean_prompt(mqkv_f32, b, t):
    nb = t // MOBA_BLOCK
    return pl.pallas_call(
        _kmean_body, grid=(b,),
        in_specs=[pl.BlockSpec((t, D_ATT), lambda i: (i, 1))],
        out_specs=pl.BlockSpec((None, nb, D_ATT), lambda i: (i, 0, 0)),
        out_shape=jax.ShapeDtypeStruct((b, nb, D_ATT), f32),
        compiler_params=_cp(("parallel",)), name="kmean_prompt")(mqkv_f32)


def _t5_tiles_body(tab_ref, o_ref):
    h = pl.program_id(0)
    key = lax.broadcasted_iota(jnp.int32, (MOBA_BLOCK, MOBA_BLOCK), 0)
    qry = lax.broadcasted_iota(jnp.int32, (MOBA_BLOCK, MOBA_BLOCK), 1)
    for which in range(2):
        bucket = _t5_bucket(jnp.maximum(qry - key + which * MOBA_BLOCK, 0))
        acc = jnp.zeros((MOBA_BLOCK, MOBA_BLOCK), f32)
        for nbk in range(N_BUCKETS):
            acc = jnp.where(bucket == nbk, tab_ref[nbk, h], acc)
        o_ref[which] = acc


def _t5_tiles(rel_bias):
    return pl.pallas_call(
        _t5_tiles_body, grid=(N_HEADS,),
        in_specs=[pl.BlockSpec(memory_space=pltpu.SMEM)],
        out_specs=pl.BlockSpec((None, 2, MOBA_BLOCK, MOBA_BLOCK), lambda h: (h, 0, 0, 0)),
        out_shape=jax.ShapeDtypeStruct((N_HEADS, 2, MOBA_BLOCK, MOBA_BLOCK), f32),
        compiler_params=_cp(("parallel",)), name="t5_tiles")(rel_bias)


def _top_blocks(gate, n_valid):
    blk = lax.broadcasted_iota(jnp.int32, gate.shape, 0)
    nb = gate.shape[0]
    g = jnp.where(blk < n_valid, gate, -jnp.inf)
    sel = jnp.zeros(gate.shape, f32)
    for _ in range(min(MOBA_TOPK, nb)):
        mx = jnp.max(g, axis=0, keepdims=True)
        idx = jnp.min(jnp.where(g == mx, blk, nb), axis=0, keepdims=True)
        pick = (blk == idx) & (mx > -jnp.inf)
        sel = jnp.where(pick, 1.0, sel)
        g = jnp.where(blk == idx, -jnp.inf, g)
    return sel


def _moba_body(tab_ref, q_ref, k_ref, vt_ref, kmean_ref, t5_ref, o_ref):
    tq = q_ref.shape[0]
    nb = kmean_ref.shape[0]
    i = pl.program_id(1)
    key = lax.broadcasted_iota(jnp.int32, (tq, tq), 0)
    qry = lax.broadcasted_iota(jnp.int32, (tq, tq), 1)
    causal = key <= qry
    blk = lax.broadcasted_iota(jnp.int32, (nb, tq), 0)
    prev = jnp.maximum(i - 1, 0)
    outs = []
    for g0 in range(0, N_HEADS, ATT_HEAD_GROUP):
        group = range(g0, g0 + ATT_HEAD_GROUP)
        qhs = [_masked_queries(q_ref, h) for h in group]
        sels = [_top_blocks(_dot_nt(kmean_ref[:, _pair_cols(h)].astype(bf16), qh), i) for h, qh in zip(group, qhs)]

        def picked(n, sel):
            return jnp.max(jnp.where(blk == n, sel, 0.0), axis=0, keepdims=True) > 0.5

        def tile(n, which, own, carries, group=group, qhs=qhs, sels=sels):
            rows = pl.ds(pl.multiple_of(n * tq, tq), tq)
            new = []
            for h, qh, sel, carry in zip(group, qhs, sels, carries):
                bias_tile = None if which is None else t5_ref[h, which]
                bias_row = tab_ref[N_BUCKETS - 1, h] if which is None else None
                mask = causal if own else picked(n, sel)
                new.append(_flash_step(qh, k_ref[rows, _pair_cols(h)], vt_ref[n, _head_rows(h), :],
                                       bias_row, None, bias_tile, mask, carry))
            return tuple(new)

        carries = tile(i, 0, True, tuple(_flash_init(tq) for _ in group))
        carries = tile(prev, 1, False, carries)
        carries = lax.fori_loop(0, prev, lambda n, cr, tile=tile: tile(n, None, False, cr), carries)
        outs += [acc / l for _, l, acc in carries]
    o_ref[...] = jnp.concatenate(outs, axis=0).T


def _moba_prompt(rel_bias, qkv, vt, kmean, t5, b, t):
    tq = MOBA_BLOCK
    nq = t // tq
    nb = kmean.shape[1]
    return pl.pallas_call(
        _moba_body, grid=(b, nq),
        in_specs=[pl.BlockSpec(memory_space=pltpu.SMEM),
                  pl.BlockSpec((tq, D_ATT), lambda i, j: (i * nq + j, 0)),
                  pl.BlockSpec((t, D_ATT), lambda i, j: (i, 1)),
                  pl.BlockSpec((nq, D_ATT, tq), lambda i, j: (i, 0, 0)),
                  pl.BlockSpec((None, nb, D_ATT), lambda i, j: (i, 0, 0)),
                  pl.BlockSpec(t5.shape, lambda i, j: (0, 0, 0, 0))],
        out_specs=pl.BlockSpec((tq, D_ATT), lambda i, j: (i * nq + j, 0)),
        out_shape=jax.ShapeDtypeStruct((b * t, D_ATT), f32),
        compiler_params=_cp(("parallel", "arbitrary")), name="moba_prompt")(rel_bias, qkv, qkv, vt, kmean, t5)


def _outproj_body(x_ref, a_ref, f_ref, m_ref, w_ref, o_ref):
    acc = _dot(a_ref[...].astype(bf16), w_ref[0:D_CONV, :])
    acc = acc + _dot(f_ref[...].astype(bf16), w_ref[D_CONV:D_CONV + D_ATT, :])
    acc = acc + _dot(m_ref[...].astype(bf16), w_ref[D_CONV + D_ATT:, :])
    o_ref[...] = x_ref[...] + acc


def _outproj(x, a, fo, mo, w):
    m = x.shape[0]
    tm = min(ROW_TILE, m)
    row = lambda wd: pl.BlockSpec((tm, wd), lambda i: (i, 0))
    return pl.pallas_call(
        _outproj_body, grid=(m // tm,),
        in_specs=[row(D_MODEL), row(D_CONV), row(D_ATT), row(D_ATT), pl.BlockSpec(w.shape, lambda i: (0, 0))],
        out_specs=row(D_MODEL), out_shape=jax.ShapeDtypeStruct(x.shape, f32),
        compiler_params=_cp(("parallel",)), name="outproj")(x, a, fo, mo, w)


def _router_body(x_ref, g_ref, w_ref, o_ref):
    logits = _dot(_rms(x_ref[...], g_ref[...]).astype(bf16), w_ref[...])
    lane = lax.broadcasted_iota(jnp.int32, logits.shape, 1)
    g = jnp.where(lane < N_EXPERTS, logits, -jnp.inf)
    v1 = jnp.max(g, axis=-1, keepdims=True)
    i1 = jnp.min(jnp.where(g == v1, lane, LANES), axis=-1, keepdims=True)
    g2 = jnp.where(lane == i1, -jnp.inf, g)
    v2 = jnp.max(g2, axis=-1, keepdims=True)
    i2 = jnp.min(jnp.where(g2 == v2, lane, LANES), axis=-1, keepdims=True)
    e2 = jnp.exp(v2 - v1)
    w1 = 1.0 / (1.0 + e2)
    w2 = e2 / (1.0 + e2)
    o_ref[...] = jnp.where(lane == i1, w1, 0.0) + jnp.where(lane == i2, w2, 0.0)


def _router(x, g, w_pad):
    m = x.shape[0]
    tm = min(ROW_TILE, m)
    return pl.pallas_call(
        _router_body, grid=(m // tm,),
        in_specs=[pl.BlockSpec((tm, D_MODEL), lambda i: (i, 0)), pl.BlockSpec(g.shape, lambda i: (0, 0)),
                  pl.BlockSpec(w_pad.shape, lambda i: (0, 0))],
        out_specs=pl.BlockSpec((tm, LANES), lambda i: (i, 0)),
        out_shape=jax.ShapeDtypeStruct((m, LANES), f32),
        compiler_params=_cp(("parallel",)), name="router")(x, g, w_pad)


def _ffn_body(has_comb, has_final, x_ref, g_ref, *rest):
    rest = list(rest)
    comb_ref = rest.pop(0) if has_comb else None
    wg_ref, wu_ref, wd_ref = rest[:3]
    rest = rest[3:]
    gfin_ref = rest.pop(0) if has_final else None
    o_ref, hn_s, acc_s = rest
    e, j = pl.program_id(1), pl.program_id(2)
    first = (e == 0) & (j == 0)
    last = (e == pl.num_programs(1) - 1) & (j == pl.num_programs(2) - 1)

    @pl.when(first)
    def _():
        hn_s[...] = _rms(x_ref[...], g_ref[...]).astype(bf16)
        acc_s[...] = jnp.zeros_like(acc_s)

    hn = hn_s[...]
    h = _silu(_dot(hn, wg_ref[...].astype(bf16))) * _dot(hn, wu_ref[...].astype(bf16))
    if has_comb:
        comb = comb_ref[...]
        lane = lax.broadcasted_iota(jnp.int32, comb.shape, 1)
        h = h * jnp.sum(jnp.where(lane == e, comb, 0.0), axis=-1, keepdims=True)
    acc_s[...] += _dot(h.astype(bf16), wd_ref[...].astype(bf16))

    @pl.when(last)
    def _():
        y = x_ref[...] + acc_s[...]
        o_ref[...] = _rms(y, gfin_ref[...]) if has_final else y


def _ffn(x, g, comb, wg, wu, wd, gfin):
    m = x.shape[0]
    n_e, _, n_f = wg.shape
    tm = min(FFN_ROW_TILE, m)
    tf = FFN_COL_TILE
    in_specs = [pl.BlockSpec((tm, D_MODEL), lambda i, e, j: (i, 0)), pl.BlockSpec(g.shape, lambda i, e, j: (0, 0))]
    args = [x, g]
    if comb is not None:
        in_specs.append(pl.BlockSpec((tm, LANES), lambda i, e, j: (i, 0)))
        args.append(comb)
    in_specs += [pl.BlockSpec((None, D_MODEL, tf), lambda i, e, j: (e, 0, j)),
                 pl.BlockSpec((None, D_MODEL, tf), lambda i, e, j: (e, 0, j)),
                 pl.BlockSpec((None, tf, D_MODEL), lambda i, e, j: (e, j, 0))]
    args += [wg, wu, wd]
    if gfin is not None:
        in_specs.append(pl.BlockSpec(gfin.shape, lambda i, e, j: (0, 0)))
        args.append(gfin)
    return pl.pallas_call(
        functools.partial(_ffn_body, comb is not None, gfin is not None),
        grid=(m // tm, n_e, n_f // tf),
        in_specs=in_specs,
        out_specs=pl.BlockSpec((tm, D_MODEL), lambda i, e, j: (i, 0)),
        out_shape=jax.ShapeDtypeStruct(x.shape, f32),
        scratch_shapes=[pltpu.VMEM((tm, D_MODEL), bf16), pltpu.VMEM((tm, D_MODEL), f32)],
        compiler_params=_cp(("parallel", "arbitrary", "arbitrary")), name="ffn")(*args)


def _route_body(comb_ref, rank_ref, rankt_ref, cnt_ref):
    comb = comb_ref[...]
    tm = comb.shape[0]
    sel = comb > 0.0
    earlier = (lax.broadcasted_iota(jnp.int32, (tm, tm), 1) < lax.broadcasted_iota(jnp.int32, (tm, tm), 0))
    before = _dot(jnp.where(earlier, 1.0, 0.0).astype(bf16), jnp.where(sel, 1.0, 0.0).astype(bf16))
    rank = jnp.where(sel, before, -1.0)
    rank_ref[...] = rank
    rankt_ref[...] = rank.T
    cnt_ref[...] = jnp.sum(jnp.where(sel, 1.0, 0.0), axis=0, keepdims=True).astype(jnp.int32)


def _route(comb):
    m = comb.shape[0]
    tm = FFN_ROW_TILE
    return pl.pallas_call(
        _route_body, grid=(m // tm,),
        in_specs=[pl.BlockSpec((tm, LANES), lambda i: (i, 0))],
        out_specs=[pl.BlockSpec((tm, LANES), lambda i: (i, 0)), pl.BlockSpec((None, LANES, tm), lambda i: (i, 0, 0)),
                   pl.BlockSpec((None, 1, LANES), lambda i: (i, 0, 0))],
        out_shape=[jax.ShapeDtypeStruct((m, LANES), f32), jax.ShapeDtypeStruct((m // tm, LANES, tm), f32),
                   jax.ShapeDtypeStruct((m // tm, 1, LANES), jnp.int32)],
        compiler_params=_cp(("parallel",)), name="route")(comb)


def _moe_body(has_final, cnt_ref, x_ref, g_ref, comb_ref, rank_ref, rankt_ref, wg_ref, wu_ref, wd_ref, *rest):
    rest = list(rest)
    gfin_ref = rest.pop(0) if has_final else None
    o_ref, hn_s, acc_s, xe_s, ye_s = rest
    tm = x_ref.shape[0]
    i, e, j = pl.program_id(0), pl.program_id(1), pl.program_id(2)
    last_j = pl.num_programs(2) - 1
    n_tok = cnt_ref[i, e]

    @pl.when((e == 0) & (j == 0))
    def _():
        hn_s[...] = _rms(x_ref[...], g_ref[...]).astype(bf16)
        acc_s[...] = jnp.zeros_like(acc_s)

    wg, wu, wd = wg_ref[...].astype(bf16), wu_ref[...].astype(bf16), wd_ref[...].astype(bf16)
    lane = lax.broadcasted_iota(jnp.int32, (1, LANES), 1)
    for c in range(tm // MOE_CHUNK):
        rows = slice(c * MOE_CHUNK, (c + 1) * MOE_CHUNK)

        @pl.when(c * MOE_CHUNK < n_tok)
        def _(c=c, rows=rows):
            @pl.when(j == 0)
            def _():
                slot = (lax.broadcasted_iota(jnp.int32, (MOE_CHUNK, 1), 0) + c * MOE_CHUNK).astype(f32)
                pick = jnp.where(rankt_ref[pl.ds(e, 1), :] == slot, 1.0, 0.0).astype(bf16)
                xe_s[rows, :] = _dot(pick, hn_s[...]).astype(bf16)
                ye_s[rows, :] = jnp.zeros((MOE_CHUNK, D_MODEL), f32)

            xe = xe_s[rows, :]
            h = _silu(_dot(xe, wg)) * _dot(xe, wu)
            ye_s[rows, :] += _dot(h.astype(bf16), wd)

            @pl.when(j == last_j)
            def _():
                rank = jnp.sum(jnp.where(lane == e, rank_ref[...], 0.0), axis=-1, keepdims=True)
                gate = jnp.sum(jnp.where(lane == e, comb_ref[...], 0.0), axis=-1, keepdims=True)
                slot = (lax.broadcasted_iota(jnp.int32, (1, MOE_CHUNK), 1) + c * MOE_CHUNK).astype(f32)
                place = jnp.where(rank == slot, 1.0, 0.0).astype(bf16)
                y = ye_s[rows, :]
                y_hi = y.astype(bf16)
                y_lo = (y - y_hi.astype(f32)).astype(bf16)
                acc_s[...] += gate * (_dot(place, y_hi) + _dot(place, y_lo))

    @pl.when((e == pl.num_programs(1) - 1) & (j == last_j))
    def _():
        y = x_ref[...] + acc_s[...]
        o_ref[...] = _rms(y, gfin_ref[...]) if has_final else y


def _moe(x, g, comb, wg, wu, wd, gfin):
    m = x.shape[0]
    n_e, _, n_f = wg.shape
    tm, tf = FFN_ROW_TILE, MOE_COL_TILE
    rank, rank_t, cnt = _route(comb)
    in_specs = [pl.BlockSpec((tm, D_MODEL), lambda i, e, j, cnt: (i, 0)), pl.BlockSpec(g.shape, lambda i, e, j, cnt: (0, 0)),
                pl.BlockSpec((tm, LANES), lambda i, e, j, cnt: (i, 0)), pl.BlockSpec((tm, LANES), lambda i, e, j, cnt: (i, 0)),
                pl.BlockSpec((None, LANES, tm), lambda i, e, j, cnt: (i, 0, 0)),
                pl.BlockSpec((None, D_MODEL, tf), lambda i, e, j, cnt: (e, 0, j)),
                pl.BlockSpec((None, D_MODEL, tf), lambda i, e, j, cnt: (e, 0, j)),
                pl.BlockSpec((None, tf, D_MODEL), lambda i, e, j, cnt: (e, j, 0))]
    args = [x, g, comb, rank, rank_t, wg, wu, wd]
    if gfin is not None:
        in_specs.append(pl.BlockSpec(gfin.shape, lambda i, e, j, cnt: (0, 0)))
        args.append(gfin)
    return pl.pallas_call(
        functools.partial(_moe_body, gfin is not None),
        grid_spec=pltpu.PrefetchScalarGridSpec(
            num_scalar_prefetch=1, grid=(m // tm, n_e, n_f // tf), in_specs=in_specs,
            out_specs=pl.BlockSpec((tm, D_MODEL), lambda i, e, j, cnt: (i, 0)),
            scratch_shapes=[pltpu.VMEM((tm, D_MODEL), bf16), pltpu.VMEM((tm, D_MODEL), f32),
                            pltpu.VMEM((tm, D_MODEL), bf16), pltpu.VMEM((tm, D_MODEL), f32)]),
        out_shape=jax.ShapeDtypeStruct(x.shape, f32),
        compiler_params=_cp(("parallel", "arbitrary", "arbitrary")), name="moe")(cnt.reshape(m // tm, LANES), *args)


def _page_spec(layer, slot, chunk=lambda c: c):
    return pl.BlockSpec((None, None, N_HEADS, HEAD_DIM, PAGE_SIZE),
                        lambda b, c, pt: (layer, pt[b, chunk(c) * DEC_PAGES + slot], 0, 0, 0))


def _logf_gather_body(pt_ref, *refs):
    o_ref = refs[-1]
    b, c = pl.program_id(0), pl.program_id(1)
    for s in range(DEC_PAGES):
        row = pt_ref[b, c * DEC_PAGES + s] % SUBLANES
        o_ref[:, pl.ds(s, 1), :] = refs[s][:, pl.ds(row, 1), :]


def _logf_gather(page_table, logf_t, layer):
    b, n_pages = page_table.shape

    def spec(slot):
        return pl.BlockSpec((None, N_HEADS, SUBLANES, PAGE_SIZE),
                            lambda i, c, pt: (layer, 0, pt[i, c * DEC_PAGES + slot] // SUBLANES, 0))

    return pl.pallas_call(
        _logf_gather_body,
        grid_spec=pltpu.PrefetchScalarGridSpec(
            num_scalar_prefetch=1, grid=(b, n_pages // DEC_PAGES),
            in_specs=[spec(s) for s in range(DEC_PAGES)],
            out_specs=pl.BlockSpec((None, N_HEADS, DEC_PAGES, PAGE_SIZE), lambda i, c, pt: (i, 0, c, 0))),
        out_shape=jax.ShapeDtypeStruct((b, N_HEADS, n_pages, PAGE_SIZE), f32),
        compiler_params=_cp(("parallel", "arbitrary")), name="logf_gather")(page_table, *([logf_t] * DEC_PAGES))


def _suffix_body(lf_ref, new_ref, o_ref):
    x = lf_ref[...]
    n = x.shape[1]
    pos = lax.broadcasted_iota(jnp.int32, x.shape, 1)
    y = x
    s = 1
    while s < n:
        y = y + jnp.where(pos + s < n, pltpu.roll(y, n - s, axis=1), 0.0)
        s *= 2
    o_ref[...] = (y - x) + new_ref[:, 0:1]


def _suffix_bias(lf_t, lf_new):
    b, _, n = lf_t.shape
    return pl.pallas_call(
        _suffix_body, grid=(b,),
        in_specs=[pl.BlockSpec((None, SUBLANES, n), lambda i: (i, 0, 0)),
                  pl.BlockSpec((None, SUBLANES, LANES), lambda i: (i, 0, 0))],
        out_specs=pl.BlockSpec((None, SUBLANES, n), lambda i: (i, 0, 0)),
        out_shape=jax.ShapeDtypeStruct(lf_t.shape, f32),
        compiler_params=_cp(("parallel",)), name="suffix_bias")(lf_t, lf_new)


def _block_diag(x):
    b = x.shape[0]
    eye = jnp.eye(SUBLANES, N_HEADS, dtype=x.dtype)
    return (eye[None, :, :, None] * x[:, None, :, :]).reshape(b, SUBLANES, D_ATT)


def _rows_of_head(x):
    eye = jnp.eye(N_HEADS, SUBLANES, dtype=x.dtype)
    return eye[None, :, :, None] * x[:, :, None, :]


def _flat_page(ref):
    return ref[...].reshape(D_ATT, PAGE_SIZE).astype(bf16)


def _fox_dec_body(pt_ref, q_ref, kn_ref, vn_ref, bias_ref, *refs):
    k_refs, v_refs = refs[:DEC_PAGES], refs[DEC_PAGES:2 * DEC_PAGES]
    o_ref, s_all, m_s, inv_s, acc_s = refs[2 * DEC_PAGES:]
    nc = s_all.shape[0]
    c = pl.program_id(1)
    q = q_ref[...]

    @pl.when(c == 0)
    def _():
        m_s[...] = jnp.full_like(m_s, -jnp.inf)

    @pl.when(c < nc)
    def _():
        qb = q.astype(bf16)
        s = jnp.concatenate([_dot(qb, _flat_page(k_refs[g])) for g in range(DEC_PAGES)], axis=1) + bias_ref[...]
        s_all[c] = s
        m_s[...] = jnp.maximum(m_s[...], jnp.max(s, axis=-1, keepdims=True))

    @pl.when(c == nc)
    def _():
        s_new = jnp.sum(_round_bf16(q) * _round_bf16(kn_ref[...]), axis=-1, keepdims=True)
        m = jnp.maximum(m_s[...], s_new)
        p_new = jnp.exp(s_new - m)
        l = p_new
        for n in range(nc):
            p = jnp.exp(s_all[n] - m)
            s_all[n] = p
            l = l + jnp.sum(p, axis=-1, keepdims=True)
        inv = 1.0 / l
        inv_s[...] = inv
        acc_s[...] = _round_bf16(p_new * inv) * _round_bf16(vn_ref[...])

    @pl.when(c >= nc)
    def _():
        p = (s_all[c - nc] * inv_s[...]).astype(bf16)
        pv = acc_s[...]
        for g in range(DEC_PAGES):
            pv = pv + _dot_nt(p[:, g * PAGE_SIZE:(g + 1) * PAGE_SIZE], _flat_page(v_refs[g]))
        acc_s[...] = pv

    @pl.when(c == 2 * nc - 1)
    def _():
        row = lax.broadcasted_iota(jnp.int32, acc_s.shape, 0)
        col = lax.broadcasted_iota(jnp.int32, acc_s.shape, 1)
        own = (col >= row * HEAD_DIM) & (col < (row + 1) * HEAD_DIM)
        o_ref[...] = jnp.sum(jnp.where(own, acc_s[...], 0.0), axis=0, keepdims=True)


def _fox_decode(page_table, q, k_new, v_new, bias, cache_k, cache_v, layer):
    b, n_pages = page_table.shape
    nc = n_pages // DEC_PAGES
    small = pl.BlockSpec((None, SUBLANES, D_ATT), lambda i, c, pt: (i, 0, 0))
    k_chunk = lambda c: jnp.minimum(c, nc - 1)
    v_chunk = lambda c: jnp.maximum(c - nc, 0)
    return pl.pallas_call(
        _fox_dec_body,
        grid_spec=pltpu.PrefetchScalarGridSpec(
            num_scalar_prefetch=1, grid=(b, 2 * nc),
            in_specs=[small, small, small,
                      pl.BlockSpec((None, SUBLANES, DEC_PAGES * PAGE_SIZE), lambda i, c, pt: (i, 0, k_chunk(c)))]
            + [_page_spec(layer, s, k_chunk) for s in range(DEC_PAGES)]
            + [_page_spec(layer, s, v_chunk) for s in range(DEC_PAGES)],
            out_specs=pl.BlockSpec((None, 1, D_ATT), lambda i, c, pt: (i, 0, 0)),
            scratch_shapes=[pltpu.VMEM((nc, SUBLANES, DEC_PAGES * PAGE_SIZE), f32), pltpu.VMEM((SUBLANES, 1), f32),
                            pltpu.VMEM((SUBLANES, 1), f32), pltpu.VMEM((SUBLANES, D_ATT), f32)]),
        out_shape=jax.ShapeDtypeStruct((b, 1, D_ATT), f32),
        compiler_params=_cp(("parallel", "arbitrary")), name="fox_decode")(
            page_table, q, k_new, v_new, bias, *([cache_k] * DEC_PAGES), *([cache_v] * DEC_PAGES))


def _gate_body(pt_ref, q_ref, *refs):
    o_ref = refs[-1]
    pages_per_block = MOBA_BLOCK // PAGE_SIZE
    q = _round_bf16(q_ref[...])
    for blk in range(DEC_PAGES // pages_per_block):
        ksum = refs[blk * pages_per_block][...]
        for s in range(1, pages_per_block):
            ksum = ksum + refs[blk * pages_per_block + s][...]
        kmean = jnp.sum(ksum, axis=-1, keepdims=True) * (1.0 / MOBA_BLOCK)
        o_ref[blk] = jnp.sum(_round_bf16(kmean) * q, axis=1)


def _gate_decode(page_table, q_col, cache_k, layer):
    b, n_pages = page_table.shape
    per_step = DEC_PAGES * PAGE_SIZE // MOBA_BLOCK
    return pl.pallas_call(
        _gate_body,
        grid_spec=pltpu.PrefetchScalarGridSpec(
            num_scalar_prefetch=1, grid=(b, n_pages // DEC_PAGES),
            in_specs=[pl.BlockSpec((None, N_HEADS, HEAD_DIM, 1), lambda i, c, pt: (i, 0, 0, 0))]
            + [_page_spec(layer, s) for s in range(DEC_PAGES)],
            out_specs=pl.BlockSpec((None, per_step, N_HEADS, 1), lambda i, c, pt: (i, c, 0, 0))),
        out_shape=jax.ShapeDtypeStruct((b, n_pages * PAGE_SIZE // MOBA_BLOCK, N_HEADS, 1), f32),
        compiler_params=_cp(("parallel", "arbitrary")), name="gate_decode")(page_table, q_col, *([cache_k] * DEC_PAGES))


def _select_body(g_ref, o_ref):
    g = g_ref[...]
    nb = g.shape[0]
    blk = lax.broadcasted_iota(jnp.int32, g.shape, 0)
    for t in range(MOBA_TOPK):
        mx = jnp.max(g, axis=0, keepdims=True)
        idx = jnp.min(jnp.where(g == mx, blk, nb), axis=0, keepdims=True)
        o_ref[t] = idx[0]
        g = jnp.where(blk == idx, -jnp.inf, g)


def _select_decode(gates):
    b, nb = gates.shape[:2]
    return pl.pallas_call(
        _select_body, grid=(b,),
        in_specs=[pl.BlockSpec((None, nb, N_HEADS, 1), lambda i: (i, 0, 0, 0))],
        out_specs=pl.BlockSpec((None, MOBA_TOPK, N_HEADS, 1), lambda i: (i, 0, 0, 0)),
        out_shape=jax.ShapeDtypeStruct((b, MOBA_TOPK, N_HEADS, 1), jnp.int32),
        compiler_params=_cp(("parallel",)), name="select_decode")(gates)


def _moba_dec_body(past_len, pt_ref, idx_ref, q_ref, kn_ref, vn_ref, tab_ref, *refs):
    pages_per_block = MOBA_BLOCK // PAGE_SIZE
    n_sel = MOBA_TOPK * pages_per_block
    k_refs, v_refs, o_ref = refs[:n_sel], refs[n_sel:2 * n_sel], refs[2 * n_sel]
    b, h = pl.program_id(0), pl.program_id(1)
    q = q_ref[...]
    qb = q.astype(bf16)
    tab = tab_ref[...]
    lane = lax.broadcasted_iota(jnp.int32, (1, PAGE_SIZE), 1)

    def t5(dist):
        bucket = _t5_bucket(dist)
        out = jnp.zeros((SUBLANES, dist.shape[1]), f32)
        for nbk in range(N_BUCKETS):
            out = jnp.where(bucket == nbk, tab[:, nbk:nbk + 1], out)
        return out

    scores = []
    for t in range(MOBA_TOPK):
        blk = idx_ref[b, t, h]
        for s in range(pages_per_block):
            kpos = blk * MOBA_BLOCK + s * PAGE_SIZE + lane
            scores.append(_dot(qb, k_refs[t * pages_per_block + s][...].astype(bf16)) + t5(past_len - kpos))
    s_past = jnp.concatenate(scores, axis=1)
    s_new = (jnp.sum(_round_bf16(q) * _round_bf16(kn_ref[...]), axis=-1, keepdims=True)
             + t5(jnp.zeros((1, 1), jnp.int32)))
    m = jnp.maximum(jnp.max(s_past, axis=-1, keepdims=True), s_new)
    p = jnp.exp(s_past - m)
    p_new = jnp.exp(s_new - m)
    inv = 1.0 / (jnp.sum(p, axis=-1, keepdims=True) + p_new)
    p = (p * inv).astype(bf16)
    acc = _round_bf16(p_new * inv) * _round_bf16(vn_ref[...])
    for g in range(n_sel):
        acc = acc + _dot_nt(p[:, g * PAGE_SIZE:(g + 1) * PAGE_SIZE], v_refs[g][...].astype(bf16))
    row = lax.broadcasted_iota(jnp.int32, (SUBLANES, HEAD_DIM), 0)
    res = jnp.where(row == h, acc, 0.0)

    @pl.when(h == 0)
    def _():
        o_ref[...] = res

    @pl.when(h > 0)
    def _():
        o_ref[...] += res


def _moba_decode(page_table, idx, q, k_new, v_new, tab_t, cache_k, cache_v, layer):
    b, n_pages = page_table.shape
    pages_per_block = MOBA_BLOCK // PAGE_SIZE
    small = pl.BlockSpec((None, None, SUBLANES, HEAD_DIM), lambda i, h, pt, ix: (i, h, 0, 0))

    def sel_spec(t, s):
        return pl.BlockSpec((None, None, None, HEAD_DIM, PAGE_SIZE),
                            lambda i, h, pt, ix: (layer, pt[i, ix[i, t, h] * pages_per_block + s], h, 0, 0))

    specs = [sel_spec(t, s) for t in range(MOBA_TOPK) for s in range(pages_per_block)]
    return pl.pallas_call(
        functools.partial(_moba_dec_body, n_pages * PAGE_SIZE),
        grid_spec=pltpu.PrefetchScalarGridSpec(
            num_scalar_prefetch=2, grid=(b, N_HEADS),
            in_specs=[small, small, small, pl.BlockSpec(tab_t.shape, lambda i, h, pt, ix: (0, 0))] + specs * 2,
            out_specs=pl.BlockSpec((None, SUBLANES, HEAD_DIM), lambda i, h, pt, ix: (i, 0, 0))),
        out_shape=jax.ShapeDtypeStruct((b, SUBLANES, HEAD_DIM), f32),
        compiler_params=_cp(("parallel", "arbitrary")), name="moba_decode")(
            page_table, idx, q, k_new, v_new, tab_t, *([cache_k] * len(specs)), *([cache_v] * len(specs)))


def _split_w_in(w):
    o_f = 2 * D_CONV
    o_g = o_f + 3 * D_ATT
    o_m = o_g + N_HEADS
    wg = jnp.zeros((D_MODEL, LANES), f32).at[:, :N_HEADS].set(w[:, o_g:o_m])
    return w[:, :o_f].astype(bf16), w[:, o_f:o_g].astype(bf16), w[:, o_m:].astype(bf16), wg.astype(bf16)


def _pad_lanes(v):
    return jnp.zeros((1, LANES), f32).at[0, :v.shape[0]].set(v)


def _layer_weights(l, p):
    wglu, wf, wm, wg = _split_w_in(p["w_in"][l])
    i = l // 2
    d = dict(norm_mix=p["norm_mix"][l][None], norm_ffn=p["norm_ffn"][l][None],
             wglu=wglu, wf=wf, wm=wm, wg=wg, bfg=_pad_lanes(p["b_forget"][l]),
             conv_w=p["conv_w"][l], conv_b=p["conv_b"][l][None], ln_g=p["conv_ln_g"][l][None], ln_b=p["conv_ln_b"][l][None],
             w_pw=p["w_conv_pw"][l].astype(bf16), w_out=p["w_out"][l].astype(bf16))
    if l % 2 == 0:
        d.update(ffn_g=p["w_dense_gate"][i][None], ffn_u=p["w_dense_up"][i][None], ffn_d=p["w_dense_down"][i][None], router=None)
    else:
        d.update(ffn_g=p["w_moe_gate"][i].astype(bf16), ffn_u=p["w_moe_up"][i].astype(bf16), ffn_d=p["w_moe_down"][i].astype(bf16),
                 router=jnp.zeros((D_MODEL, LANES), f32).at[:, :N_EXPERTS].set(p["w_router"][i]).astype(bf16))
    return d


def _channel_mix(x, w, gfin):
    comb = None if w["router"] is None else _router(x, w["norm_ffn"], w["router"])
    if comb is not None and x.shape[0] % FFN_ROW_TILE == 0:
        return _moe(x, w["norm_ffn"], comb, w["ffn_g"], w["ffn_u"], w["ffn_d"], gfin)
    return _ffn(x, w["norm_ffn"], comb, w["ffn_g"], w["ffn_u"], w["ffn_d"], gfin)


def _heads(a, b, t):
    return a.reshape(b, t, N_HEADS, HEAD_DIM)


def _prompt_layer(x, w, rel_bias, t5, b, t, gfin):
    u, ff, fm, bff, bfm, lf, lft, fvt, mvt = _proj(x, w["norm_mix"], w["wglu"], w["wf"], w["wm"], w["wg"], w["bfg"], True)
    u3 = u.reshape(b, t, D_CONV)
    up = jnp.concatenate([jnp.zeros((b, CONV_HALO, D_CONV), f32), u3], axis=1)
    conv_out = _conv_prompt(up, w["conv_w"], w["conv_b"], w["ln_g"], w["ln_b"], w["w_pw"], t).reshape(b * t, D_CONV)
    ccol, crow = _cumsum(lf, lft, b, t)
    nq = t // min(ATT_TILE, t)
    crow3 = crow.reshape(SUBLANES, b * nq, t // nq).transpose(1, 0, 2)
    fox_out = _fox_prompt(bff, fvt, ccol, crow3, b, t)
    kmean = _kmean_prompt(fm, b, t)
    moba_out = _moba_prompt(rel_bias, bfm, mvt, kmean, t5, b, t)
    x = _outproj(x, conv_out, fox_out, moba_out, w["w_out"])
    x = _channel_mix(x, w, gfin)
    state = (_heads(ff[:, D_ATT:2 * D_ATT], b, t), _heads(ff[:, 2 * D_ATT:], b, t), lf[:, :N_HEADS].reshape(b, t, N_HEADS),
             _heads(fm[:, D_ATT:2 * D_ATT], b, t), _heads(fm[:, 2 * D_ATT:], b, t), u3[:, t - (CONV_WIDTH - 1):])
    return x, state


def _sample_layer(x, w, l, rel_bias, caches, page_table, gfin):
    cache_fk, cache_fv, logf_t, cache_mk, cache_mv, state_t = caches
    b = x.shape[0]
    n_past = page_table.shape[1] * PAGE_SIZE
    u, ff, fm, _, _, lf = _proj(x, w["norm_mix"], w["wglu"], w["wf"], w["wm"], w["wg"], w["bfg"], False)
    conv_out = _conv_step(state_t[l], u, w["conv_w"], w["conv_b"], w["ln_g"], w["ln_b"], w["w_pw"])
    heads = lambda a: a.reshape(b, N_HEADS, HEAD_DIM)
    fq, fk, fv = heads(ff[:, :D_ATT]) * Q_SCALE, heads(ff[:, D_ATT:2 * D_ATT]), heads(ff[:, 2 * D_ATT:])
    mq, mk, mv = heads(fm[:, :D_ATT]) * Q_SCALE, heads(fm[:, D_ATT:2 * D_ATT]), heads(fm[:, 2 * D_ATT:])
    lf_pages = _logf_gather(page_table, logf_t, l)
    lf_t = jnp.pad(lf_pages.reshape(b, N_HEADS, n_past), ((0, 0), (0, SUBLANES - N_HEADS), (0, 0)))
    lf_new = jnp.broadcast_to(jnp.pad(lf[:, :N_HEADS], ((0, 0), (0, SUBLANES - N_HEADS)))[:, :, None], (b, SUBLANES, LANES))
    bias = _suffix_bias(lf_t, lf_new)
    fox_out = _fox_decode(page_table, _block_diag(fq), _block_diag(fk), _block_diag(fv), bias,
                          cache_fk, cache_fv, l).reshape(b, D_ATT)
    idx = _select_decode(_gate_decode(page_table, mq[..., None], cache_mk, l))[..., 0]
    tab_t = jnp.pad(rel_bias.T, ((0, SUBLANES - N_HEADS), (0, 0)))
    moba_out = _moba_decode(page_table, idx, _rows_of_head(mq), _rows_of_head(mk), _rows_of_head(mv), tab_t,
                            cache_mk, cache_mv, l)[:, :N_HEADS].reshape(b, D_ATT)
    x = _outproj(x, conv_out, fox_out, moba_out, w["w_out"])
    x = _channel_mix(x, w, gfin)
    conv_state_t = jnp.concatenate([state_t[l][1:], u[None]], axis=0)
    state = (fk[:, None], fv[:, None], lf[:, None, :N_HEADS], mk[:, None], mv[:, None], conv_state_t)
    return x, state


def kernel(x_prompt, x_sample, cache_fox_k, cache_fox_v, cache_fox_logf, cache_moba_k, cache_moba_v, state_conv, page_table, norm_mix, norm_ffn, norm_final, w_in, b_forget, conv_w, conv_b, conv_ln_g, conv_ln_b, w_conv_pw, rel_bias, w_out, w_dense_gate, w_dense_up, w_dense_down, w_router, w_moe_gate, w_moe_up, w_moe_down):
    p = dict(norm_mix=norm_mix, norm_ffn=norm_ffn, w_in=w_in, b_forget=b_forget, conv_w=conv_w, conv_b=conv_b,
             conv_ln_g=conv_ln_g, conv_ln_b=conv_ln_b, w_conv_pw=w_conv_pw, w_out=w_out, w_dense_gate=w_dense_gate,
             w_dense_up=w_dense_up, w_dense_down=w_dense_down, w_router=w_router, w_moe_gate=w_moe_gate,
             w_moe_up=w_moe_up, w_moe_down=w_moe_down)
    depth = norm_mix.shape[0]
    weights = [_layer_weights(l, p) for l in range(depth)]
    gfin = norm_final[None]
    t5 = _t5_tiles(rel_bias)

    b, t, _ = x_prompt.shape
    x = x_prompt.reshape(b * t, D_MODEL)
    states_p = []
    for l in range(depth):
        x, st = _prompt_layer(x, weights[l], rel_bias, t5, b, t, gfin if l == depth - 1 else None)
        states_p.append(st)
    y_prompt = x.reshape(b, t, D_MODEL)

    bs = x_sample.shape[0]
    rows_on_lanes = lambda c: jnp.transpose(c, (0, 1, 3, 4, 2))
    caches = (rows_on_lanes(cache_fox_k), rows_on_lanes(cache_fox_v), jnp.transpose(cache_fox_logf, (0, 3, 1, 2)),
              rows_on_lanes(cache_moba_k), rows_on_lanes(cache_moba_v), jnp.transpose(state_conv, (0, 2, 1, 3)))
    x = x_sample.reshape(bs, D_MODEL)
    states_s = []
    for l in range(depth):
        x, st = _sample_layer(x, weights[l], l, rel_bias, caches, page_table, gfin if l == depth - 1 else None)
        states_s.append(st)
    y_sample = x.reshape(bs, 1, D_MODEL)

    stack = lambda sts: [jnp.stack(s) for s in zip(*sts)]
    out_s = stack(states_s)
    out_s[-1] = jnp.transpose(out_s[-1], (0, 2, 1, 3))
    return (y_prompt, y_sample, *stack(states_p), *out_s)
```
